```python
import math, functools
import jax, jax.numpy as jnp
from jax import lax
import numpy as np

D_MODEL = 1024
BATCH = 16
SEQ = 4096
DEPTH = 1
DEC_BATCH = 8
DEC_SEQ = 16
PAST_LEN = 2048

CHUNK = 64
HEAD_DIM = 64
DIFF_HEADS = 4
DIFF_V_DIM = 2 * HEAD_DIM
DSA_HEADS = 8
IDX_HEADS = 4
IDX_DIM = 64
TOPK_MAX = 256
D_FF = 2816
NUM_BUCKETS = 32
MAX_DISTANCE = 128
Q_BLOCK = 128
EPS = 1e-6
IDX_SCALE = (IDX_HEADS ** -0.5) * (IDX_DIM ** -0.5)

DIFF_QK = DIFF_HEADS * 2 * HEAD_DIM
DIFF_WIDTH = DIFF_HEADS * DIFF_V_DIM
DSA_WIDTH = DSA_HEADS * HEAD_DIM
MIX_WIDTH = DIFF_WIDTH + DSA_WIDTH
N_BIAS_HEADS = DIFF_HEADS + DSA_HEADS
IN_SIZES = (DIFF_QK, DIFF_QK, DIFF_WIDTH, DSA_WIDTH, HEAD_DIM, HEAD_DIM, IDX_HEADS * IDX_DIM, IDX_DIM, IDX_HEADS)
IN_COLS = sum(IN_SIZES)

kernel_name = 'hybrid_diffattn_dsa_macaron_stream_step'


def _rmsnorm(x, g):
    xf = x.astype(jnp.float32)
    y = xf * lax.rsqrt(jnp.mean(xf * xf, axis=-1, keepdims=True) + EPS)
    return (y * g.astype(jnp.float32)).astype(x.dtype)


def _half_ffn(x, g, w_gate, w_up, w_down):
    h = _rmsnorm(x, g)
    return x + 0.5 * ((jax.nn.silu(h @ w_gate) * (h @ w_up)) @ w_down)


def _t5_bucket(rel):
    half = NUM_BUCKETS // 2
    exact = half // 2
    n = jnp.abs(rel)
    logn = jnp.log(jnp.maximum(n, 1).astype(jnp.float32) / exact)
    large = exact + (logn / math.log(MAX_DISTANCE / exact) * (half - exact)).astype(jnp.int32)
    large = jnp.minimum(large, half - 1)
    return jnp.where(rel > 0, half, 0) + jnp.where(n < exact, n, large)


def _visible(q_pos, k_pos):
    return (k_pos[None, :] // CHUNK) <= (q_pos[:, None] // CHUNK)


def _project(h, w_in, g_dq, g_dk, g_sq, g_sk):
    B, T, _ = h.shape
    cuts = np.cumsum(IN_SIZES)[:-1].tolist()
    dq, dk, dv, sq, sk, sv, iq, ik, iw = jnp.split(h @ w_in, cuts, axis=-1)
    dq = _rmsnorm(dq.reshape(B, T, DIFF_HEADS, 2, HEAD_DIM), g_dq)
    dk = _rmsnorm(dk.reshape(B, T, DIFF_HEADS, 2, HEAD_DIM), g_dk)
    dv = dv.reshape(B, T, DIFF_HEADS, DIFF_V_DIM)
    sq = _rmsnorm(sq.reshape(B, T, DSA_HEADS, HEAD_DIM), g_sq)
    sk = _rmsnorm(sk, g_sk)
    iq = iq.reshape(B, T, IDX_HEADS, IDX_DIM)
    return (dq, sq, iq, iw), (dk, dv, sk, sv, ik)


def _diff_attention(q, k, v, q_pos, k_pos, lam, lam_init, g_subln, bias_table):
    B, Tq = q.shape[:2]
    s = jnp.einsum('bqhcd,bshcd->bhcqs', q, k).astype(jnp.float32) * HEAD_DIM ** -0.5
    bias = bias_table[_t5_bucket(k_pos[None, :] - q_pos[:, None])]
    s = s + jnp.transpose(bias, (2, 0, 1)).astype(jnp.float32)[None, :, None]
    s = jnp.where(_visible(q_pos, k_pos), s, -jnp.inf)
    p = jax.nn.softmax(s, axis=-1)
    p = p[:, :, 0] - lam * p[:, :, 1]
    o = jnp.einsum('bhqs,bshv->bqhv', p.astype(v.dtype), v)
    o = _rmsnorm(o, g_subln) * (1.0 - lam_init)
    return o.reshape(B, Tq, DIFF_WIDTH)


def _dsa_attention(q, k, v, iq, iw, ik, q_pos, k_pos, bias_table):
    B, Tq = q.shape[:2]
    n_sel = min(TOPK_MAX, k.shape[1] // 4)
    rel_idx = jax.nn.relu(jnp.einsum('bqhd,bsd->bqhs', iq, ik).astype(jnp.float32))
    score = jnp.einsum('bqh,bqhs->bqs', iw.astype(jnp.float32) * IDX_SCALE, rel_idx)
    score = jnp.where(_visible(q_pos, k_pos), score, -jnp.inf)
    top_val, top_idx = lax.top_k(score, n_sel)
    take = jax.vmap(lambda rows, idx: rows[idx])
    k_sel = take(k, top_idx)
    v_sel = take(v, top_idx)
    s = jnp.einsum('bqhd,bqkd->bqhk', q, k_sel).astype(jnp.float32) * HEAD_DIM ** -0.5
    bias = bias_table[_t5_bucket(k_pos[top_idx] - q_pos[None, :, None])]
    s = s + jnp.swapaxes(bias, -1, -2).astype(jnp.float32)
    s = jnp.where(jnp.isfinite(top_val)[:, :, None, :], s, -jnp.inf)
    p = jax.nn.softmax(s, axis=-1)
    o = jnp.einsum('bqhk,bqkd->bqhd', p.astype(v.dtype), v_sel)
    return o.reshape(B, Tq, DSA_WIDTH)


def _token_mix(dq, sq, iq, iw, q_pos, dk, dv, sk, sv, ik, k_pos, lam, lam_init, g_subln, rel_bias):
    o_a = _diff_attention(dq, dk, dv, q_pos, k_pos, lam, lam_init, g_subln, rel_bias[:, :DIFF_HEADS])
    o_b = _dsa_attention(sq, sk, sv, iq, iw, ik, q_pos, k_pos, rel_bias[:, DIFF_HEADS:])
    return jnp.concatenate([o_a, o_b], axis=-1)


def _query_blocks(fn, q_tensors, q_pos):
    nb = q_pos.shape[0] // Q_BLOCK
    def split(a):
        return jnp.moveaxis(a.reshape(a.shape[0], nb, Q_BLOCK, *a.shape[2:]), 1, 0)
    xs = (tuple(split(a) for a in q_tensors), q_pos.reshape(nb, Q_BLOCK))
    out = lax.map(lambda blk: fn(blk[0], blk[1]), xs)
    out = jnp.moveaxis(out, 0, 1)
    return out.reshape(out.shape[0], nb * Q_BLOCK, out.shape[-1])


def setup_inputs(seed: int = 0) -> dict:
    key = jax.random.key(seed)
    ks = iter(jax.random.split(key, 32))
    nrm = lambda shape, scale: scale * jax.random.normal(next(ks), shape, jnp.float32)
    gain = lambda shape: 1.0 + 0.05 * jax.random.normal(next(ks), shape, jnp.float32)
    return {
        'x_prompt': nrm((BATCH, SEQ, D_MODEL), 1.0),
        'x_sample': nrm((DEC_BATCH, DEC_SEQ, D_MODEL), 1.0),
        'cache_diff_k': nrm((DEPTH, DEC_BATCH, PAST_LEN, DIFF_HEADS, 2, HEAD_DIM), 1.0),
        'cache_diff_v': nrm((DEPTH, DEC_BATCH, PAST_LEN, DIFF_HEADS, DIFF_V_DIM), 1.0),
        'cache_dsa_k': nrm((DEPTH, DEC_BATCH, PAST_LEN, HEAD_DIM), 1.0),
        'cache_dsa_v': nrm((DEPTH, DEC_BATCH, PAST_LEN, HEAD_DIM), 1.0),
        'cache_idx_k': nrm((DEPTH, DEC_BATCH, PAST_LEN, IDX_DIM), 1.0),
        'g_ffn1': gain((DEPTH, D_MODEL)),
        'w1_gate': nrm((DEPTH, D_MODEL, D_FF), D_MODEL ** -0.5),
        'w1_up': nrm((DEPTH, D_MODEL, D_FF), D_MODEL ** -0.5),
        'w1_down': nrm((DEPTH, D_FF, D_MODEL), D_FF ** -0.5),
        'g_mix': gain((DEPTH, D_MODEL)),
        'w_in': nrm((DEPTH, D_MODEL, IN_COLS), D_MODEL ** -0.5),
        'g_dq': gain((DEPTH, HEAD_DIM)),
        'g_dk': gain((DEPTH, HEAD_DIM)),
        'g_sq': gain((DEPTH, HEAD_DIM)),
        'g_sk': gain((DEPTH, HEAD_DIM)),
        'lam_q1': nrm((DEPTH, HEAD_DIM), 0.1),
        'lam_k1': nrm((DEPTH, HEAD_DIM), 0.1),
        'lam_q2': nrm((DEPTH, HEAD_DIM), 0.1),
        'lam_k2': nrm((DEPTH, HEAD_DIM), 0.1),
        'g_subln': gain((DEPTH, DIFF_V_DIM)),
        'w_out': nrm((DEPTH, MIX_WIDTH, D_MODEL), MIX_WIDTH ** -0.5),
        'g_ffn2': gain((DEPTH, D_MODEL)),
        'w2_gate': nrm((DEPTH, D_MODEL, D_FF), D_MODEL ** -0.5),
        'w2_up': nrm((DEPTH, D_MODEL, D_FF), D_MODEL ** -0.5),
        'w2_down': nrm((DEPTH, D_FF, D_MODEL), D_FF ** -0.5),
        'rel_bias': nrm((NUM_BUCKETS, N_BIAS_HEADS), 0.5),
    }


def reference(x_prompt, x_sample, cache_diff_k, cache_diff_v, cache_dsa_k, cache_dsa_v, cache_idx_k,
              g_ffn1, w1_gate, w1_up, w1_down, g_mix, w_in, g_dq, g_dk, g_sq, g_sk,
              lam_q1, lam_k1, lam_q2, lam_k2, g_subln, w_out, g_ffn2, w2_gate, w2_up, w2_down, rel_bias):
    seq_p = x_prompt.shape[1]
    seq_s = x_sample.shape[1]
    past = cache_diff_k.shape[2]
    pos_p = jnp.arange(seq_p, dtype=jnp.int32)
    pos_s = past + jnp.arange(seq_s, dtype=jnp.int32)
    pos_all = jnp.arange(past + seq_s, dtype=jnp.int32)
    caches = (cache_diff_k, cache_diff_v, cache_dsa_k, cache_dsa_v, cache_idx_k)
    xp, xs = x_prompt, x_sample
    new_p, new_s = [], []
    for l in range(DEPTH):
        lam_init = 0.8 - 0.6 * math.exp(-0.3 * l)
        lam = (jnp.exp(jnp.sum(lam_q1[l].astype(jnp.float32) * lam_k1[l].astype(jnp.float32)))
               - jnp.exp(jnp.sum(lam_q2[l].astype(jnp.float32) * lam_k2[l].astype(jnp.float32)))
               + lam_init)
        mix = functools.partial(_token_mix, lam=lam, lam_init=lam_init, g_subln=g_subln[l], rel_bias=rel_bias)
        proj = functools.partial(_project, w_in=w_in[l], g_dq=g_dq[l], g_dk=g_dk[l], g_sq=g_sq[l], g_sk=g_sk[l])

        xp = _half_ffn(xp, g_ffn1[l], w1_gate[l], w1_up[l], w1_down[l])
        q_p, kv_p = proj(_rmsnorm(xp, g_mix[l]))
        o_p = _query_blocks(lambda qb, pb: mix(*qb, pb, *kv_p, pos_p), q_p, pos_p)
        xp = xp + o_p @ w_out[l]
        xp = _half_ffn(xp, g_ffn2[l], w2_gate[l], w2_up[l], w2_down[l])

        xs = _half_ffn(xs, g_ffn1[l], w1_gate[l], w1_up[l], w1_down[l])
        q_s, kv_s = proj(_rmsnorm(xs, g_mix[l]))
        kv_all = [jnp.concatenate([c[l], n], axis=1) for c, n in zip(caches, kv_s)]
        o_s = mix(*q_s, pos_s, *kv_all, pos_all)
        xs = xs + o_s @ w_out[l]
        xs = _half_ffn(xs, g_ffn2[l], w2_gate[l], w2_up[l], w2_down[l])

        new_p.append(kv_p)
        new_s.append(kv_s)
    dk_p, dv_p, sk_p, sv_p, ik_p = [jnp.stack(t) for t in zip(*new_p)]
    dk_s, dv_s, sk_s, sv_s, ik_s = [jnp.stack(t) for t in zip(*new_s)]
    return (xp, xs, dk_p, dv_p, sk_p, sv_p, ik_p, dk_s, dv_s, sk_s, sv_s, ik_s)
```

```python
import functools
import math

import jax
import jax.numpy as jnp
from jax import lax
from jax.experimental import pallas as pl
from jax.experimental.pallas import tpu as pltpu

F32 = jnp.float32
BF16 = jnp.bfloat16
I32 = jnp.int32

CHUNK = 64
HEAD_DIM = 64
DIFF_HEADS = 4
DIFF_V_DIM = 2 * HEAD_DIM
DSA_HEADS = 8
IDX_HEADS = 4
IDX_DIM = 64
TOPK_MAX = 256
NUM_BUCKETS = 32
MAX_DISTANCE = 128
EPS = 1e-6
IDX_SCALE = (IDX_HEADS ** -0.5) * (IDX_DIM ** -0.5)
QK_SCALE = HEAD_DIM ** -0.5

DIFF_QK = DIFF_HEADS * 2 * HEAD_DIM
DIFF_WIDTH = DIFF_HEADS * DIFF_V_DIM
DSA_WIDTH = DSA_HEADS * HEAD_DIM
IN_COLS = 3 * DIFF_QK + DSA_WIDTH + 2 * HEAD_DIM + IDX_HEADS * IDX_DIM + IDX_DIM + IDX_HEADS

LANES = 128
BF16_SUBLANES = 16
MXU_COLS = 256
VMEM_LIMIT_BYTES = 60000 * 1024

IN_COLS_PAD = -(-IN_COLS // MXU_COLS) * MXU_COLS
ONES_ROWS = BF16_SUBLANES
VA_ROWS = DIFF_V_DIM + ONES_ROWS
VB_ROWS = HEAD_DIM + ONES_ROWS

KEY_TILE = 256
Q_TILE = 256
M_INIT = -1e30
S_MASKED = -2e30
INT_MIN = -(2 ** 31)

_C_DQ, _C_DK, _C_DV, _C_SQ = 0, DIFF_QK, 2 * DIFF_QK, 3 * DIFF_QK
_C_SKV = _C_SQ + DSA_WIDTH
_C_IQ = _C_SKV + 2 * HEAD_DIM
_C_IKW = _C_IQ + IDX_HEADS * IDX_DIM


def _params(n_axes):
    return pltpu.CompilerParams(dimension_semantics=("arbitrary",) * n_axes,
                                vmem_limit_bytes=VMEM_LIMIT_BYTES)


def _const_spec(shape):
    nd = len(shape)
    return pl.BlockSpec(shape, lambda *_: (0,) * nd, pipeline_mode=pl.Buffered(1))


def _rms(x, g):
    return x * lax.rsqrt(jnp.mean(x * x, axis=-1, keepdims=True) + EPS) * g


def _dot(a, b):
    return jnp.dot(a, b, preferred_element_type=F32)


def _dot_nt(a, b):
    return lax.dot_general(a, b, (((1,), (1,)), ((), ())), preferred_element_type=F32)


def _ff_chunks(d_ff):
    step = 4 * MXU_COLS
    return [(s, min(step, d_ff - s)) for s in range(0, d_ff, step)]


def _swiglu_half(x, g_ref, wg_ref, wu_ref, wd_ref):
    h = _rms(x, g_ref[...]).astype(BF16)
    acc = None
    for start, width in _ff_chunks(wg_ref.shape[1]):
        gate = _dot(h, wg_ref[:, start:start + width])
        up = _dot(h, wu_ref[:, start:start + width])
        act = (gate * (1.0 / (1.0 + jnp.exp(-gate))) * up).astype(BF16)
        part = _dot(act, wd_ref[start:start + width, :])
        acc = part if acc is None else acc + part
    return x + 0.5 * acc


def _ffn_kernel(x_ref, g_ref, wg_ref, wu_ref, wd_ref, o_ref):
    o_ref[...] = _swiglu_half(x_ref[...], g_ref, wg_ref, wu_ref, wd_ref)


def _out_ffn_kernel(x_ref, oa_ref, ob_ref, woa_ref, wob_ref, g_ref, wg_ref, wu_ref, wd_ref, o_ref):
    x = x_ref[...] + _dot(oa_ref[...], woa_ref[...]) + _dot(ob_ref[...], wob_ref[...])
    o_ref[...] = _swiglu_half(x, g_ref, wg_ref, wu_ref, wd_ref)


def _row_tile(n):
    for tm in (512, 256, 128, 64, 32, 16, 8):
        if n % tm == 0:
            return tm
    raise ValueError(f"row count {n} is not a multiple of 8")


def _ffn(x, g, wg, wu, wd):
    n, d = x.shape
    tm = _row_tile(n)
    row = pl.BlockSpec((tm, d), lambda i: (i, 0))
    return pl.pallas_call(
        _ffn_kernel, grid=(n // tm,),
        in_specs=[row, _const_spec(g.shape), _const_spec(wg.shape), _const_spec(wu.shape), _const_spec(wd.shape)],
        out_specs=row, out_shape=jax.ShapeDtypeStruct((n, d), F32),
        compiler_params=_params(1), name="ffn")(x, g, wg, wu, wd)


def _out_ffn(x, oa, ob, woa, wob, g, wg, wu, wd):
    n, d = x.shape
    tm = _row_tile(n)
    row = pl.BlockSpec((tm, d), lambda i: (i, 0))
    half = pl.BlockSpec((tm, oa.shape[1]), lambda i: (i, 0))
    consts = [woa, wob, g, wg, wu, wd]
    return pl.pallas_call(
        _out_ffn_kernel, grid=(n // tm,),
        in_specs=[row, half, half] + [_const_spec(c.shape) for c in consts],
        out_specs=row, out_shape=jax.ShapeDtypeStruct((n, d), F32),
        compiler_params=_params(1), name="out_ffn")(x, oa, ob, *consts)


def _dup_halves(t, first):
    lane = lax.broadcasted_iota(I32, t.shape, 1)
    rolled = pltpu.roll(t, HEAD_DIM, axis=1)
    if first:
        return jnp.where(lane < HEAD_DIM, t, rolled)
    return jnp.where(lane < HEAD_DIM, rolled, t)


def _proj_kernel(x_ref, g_ref, win_ref, gdq_ref, gdk_ref, gsq_ref, gsk_ref, g512_ref, g128_ref,
                 dkc_ref, dvc_ref, skc_ref, svc_ref, ikc_ref,
                 dq_ref, dkb_ref, dvt_ref, sq_ref, skk_ref, svt_ref, iq_ref, ikk_ref, iwt_ref):
    h = _rms(x_ref[...], g_ref[...]).astype(BF16)
    z = _dot(h, win_ref[...])
    n_sub, _, t_sub = dvt_ref.shape

    def headnorm(seg, ones_ref, gain):
        ss = _dot((seg * seg).astype(BF16), ones_ref[...])
        return seg * lax.rsqrt(ss * (1.0 / HEAD_DIM) + EPS) * gain

    dq_ref[...] = headnorm(z[:, _C_DQ:_C_DQ + DIFF_QK], g512_ref, gdq_ref[...]).astype(BF16)
    dk = headnorm(z[:, _C_DK:_C_DK + DIFF_QK], g512_ref, gdk_ref[...])
    dkc_ref[...] = dk
    dkb_ref[...] = dk.astype(BF16)
    sq_ref[...] = headnorm(z[:, _C_SQ:_C_SQ + DSA_WIDTH], g512_ref, gsq_ref[...]).astype(BF16)

    dv = z[:, _C_DV:_C_DV + DIFF_WIDTH]
    dvc_ref[...] = dv
    ones_a = jnp.ones((ONES_ROWS, t_sub), BF16)
    for s in range(n_sub):
        for hh in range(DIFF_HEADS):
            blk = dv[s * t_sub:(s + 1) * t_sub, hh * DIFF_V_DIM:(hh + 1) * DIFF_V_DIM]
            dvt_ref[s, hh * VA_ROWS:hh * VA_ROWS + DIFF_V_DIM, :] = blk.T.astype(BF16)
            dvt_ref[s, hh * VA_ROWS + DIFF_V_DIM:(hh + 1) * VA_ROWS, :] = ones_a

    skv = z[:, _C_SKV:_C_SKV + LANES]
    skn = headnorm(skv, g128_ref, gsk_ref[...])
    skk = _dup_halves(skn, first=True)
    svv = _dup_halves(skv, first=False)
    skc_ref[...] = skk[:, :HEAD_DIM]
    svc_ref[...] = svv[:, :HEAD_DIM]
    skk_ref[...] = skk.astype(BF16)
    for s in range(n_sub):
        svt_ref[s, 0:HEAD_DIM, :] = svv[s * t_sub:(s + 1) * t_sub, :].T[:HEAD_DIM, :].astype(BF16)
        svt_ref[s, HEAD_DIM:VB_ROWS, :] = ones_a

    iq_ref[...] = z[:, _C_IQ:_C_IQ + IDX_HEADS * IDX_DIM].astype(BF16)
    ikw = z[:, _C_IKW:_C_IKW + LANES]
    ikk = _dup_halves(ikw, first=True)
    ikc_ref[...] = ikk[:, :HEAD_DIM]
    ikk_ref[...] = ikk.astype(BF16)
    iwt_ref[...] = ikw.T[HEAD_DIM:HEAD_DIM + 8, :] * IDX_SCALE


def _proj(x1, g_mix, w_in, g_dq, g_dk, g_sq, g_sk):
    n, d = x1.shape
    tm = _row_tile(n)
    t_sub = min(KEY_TILE, tm)
    n_sub = tm // t_sub
    nt = n // t_sub

    def tile8(g, scale):
        return (jnp.tile(g.astype(F32), DIFF_QK // HEAD_DIM) * scale)[None, :]

    gsk_pad = jnp.concatenate([g_sk.astype(F32), jnp.ones((HEAD_DIM,), F32)])[None, :]
    grp = jnp.arange(DIFF_QK) // HEAD_DIM
    ones512 = (grp[:, None] == grp[None, :]).astype(BF16)
    ones128 = ones512[:LANES, :LANES]
    consts = [g_mix, w_in, tile8(g_dq, QK_SCALE), tile8(g_dk, 1.0), tile8(g_sq, QK_SCALE), gsk_pad,
              ones512, ones128]

    def rows(width):
        return pl.BlockSpec((tm, width), lambda i: (i, 0))

    def tposed(nrows):
        return pl.BlockSpec((n_sub, nrows, t_sub), lambda i: (i, 0, 0))

    out_shapes = [
        jax.ShapeDtypeStruct((n, DIFF_QK), F32), jax.ShapeDtypeStruct((n, DIFF_WIDTH), F32),
        jax.ShapeDtypeStruct((n, HEAD_DIM), F32), jax.ShapeDtypeStruct((n, HEAD_DIM), F32),
        jax.ShapeDtypeStruct((n, IDX_DIM), F32),
        jax.ShapeDtypeStruct((n, DIFF_QK), BF16), jax.ShapeDtypeStruct((n, DIFF_QK), BF16),
        jax.ShapeDtypeStruct((nt, DIFF_HEADS * VA_ROWS, t_sub), BF16),
        jax.ShapeDtypeStruct((n, DSA_WIDTH), BF16), jax.ShapeDtypeStruct((n, LANES), BF16),
        jax.ShapeDtypeStruct((nt, VB_ROWS, t_sub), BF16),
        jax.ShapeDtypeStruct((n, IDX_HEADS * IDX_DIM), BF16), jax.ShapeDtypeStruct((n, LANES), BF16),
        jax.ShapeDtypeStruct((8, n), F32),
    ]
    out_specs = [rows(DIFF_QK), rows(DIFF_WIDTH), rows(HEAD_DIM), rows(HEAD_DIM), rows(IDX_DIM),
                 rows(DIFF_QK), rows(DIFF_QK), tposed(DIFF_HEADS * VA_ROWS),
                 rows(DSA_WIDTH), rows(LANES), tposed(VB_ROWS),
                 rows(IDX_HEADS * IDX_DIM), rows(LANES), pl.BlockSpec((8, tm), lambda i: (0, i))]
    outs = pl.pallas_call(
        _proj_kernel, grid=(n // tm,),
        in_specs=[rows(d)] + [_const_spec(c.shape) for c in consts],
        out_specs=out_specs, out_shape=out_shapes,
        compiler_params=_params(1), name="proj")(x1, *consts)
    names = ("dk_c", "dv_c", "sk_c", "sv_c", "ik_c", "dq", "dk", "dvt", "sq", "skk", "svt", "iq", "ikk", "iwt")
    return dict(zip(names, outs))


def _half_masked_weights(q_tile, n_pairs):
    lane = lax.broadcasted_iota(I32, (q_tile.shape[0], LANES), 1)
    parts = []
    for pr in range(n_pairs):
        qp = q_tile[:, pr * LANES:(pr + 1) * LANES].astype(F32)
        parts.append(jnp.where(lane < HEAD_DIM, qp, 0.0).astype(q_tile.dtype))
        parts.append(jnp.where(lane >= HEAD_DIM, qp, 0.0).astype(q_tile.dtype))
    return jnp.concatenate(parts, axis=0)


def _softmax_step(s, m_old):
    m_new = jnp.maximum(m_old, jnp.max(s, axis=0, keepdims=True))
    alpha = jnp.exp(m_old - m_new)
    p = jnp.exp(s - m_new).astype(BF16)
    return m_new, alpha, p


def _attn_a_kernel(lamv_ref, q_ref, k_ref, vt_ref, bias_ref, gsub_ref, o_ref, m_ref, acc_ref,
                   *, q_tile_index, lam_init):
    tq = q_ref.shape[0]
    tk = vt_ref.shape[2]
    i = pl.program_id(1) if q_tile_index is None else q_tile_index
    weights = _half_masked_weights(q_ref[...], DIFF_HEADS)
    m_ref[...] = jnp.full(m_ref.shape, M_INIT, F32)
    acc_ref[...] = jnp.zeros(acc_ref.shape, F32)

    def step(j, bias_idx):
        row0 = pl.multiple_of(j * tk, tk)
        for h in range(DIFF_HEADS):
            kt = k_ref[pl.ds(row0, tk), h * LANES:(h + 1) * LANES]
            s = _dot_nt(kt, weights[2 * h * tq:(2 * h + 2) * tq, :])
            if bias_idx is not None:
                b = bias_ref[h, bias_idx]
                s = s + jnp.concatenate([b, b], axis=1)
            m_new, alpha, p = _softmax_step(s, m_ref[h])
            pv = _dot(vt_ref[j, h * VA_ROWS:(h + 1) * VA_ROWS, :], p)
            acc_ref[h] = alpha * acc_ref[h] + pv
            m_ref[h] = m_new

    def far_body(j, carry):
        step(j, None)
        return carry

    if q_tile_index is None:
        lax.fori_loop(0, jnp.maximum(i - 1, 0), far_body, 0)
        pl.when(i >= 1)(lambda: step(i - 1, 0))
    else:
        lax.fori_loop(0, max(i - 1, 0), far_body, 0)
        if i >= 1:
            step(i - 1, 0)
    step(i, 1)

    lv = lamv_ref[...]
    lam = (jnp.exp(jnp.sum(lv[0:1] * lv[1:2], axis=1, keepdims=True))
           - jnp.exp(jnp.sum(lv[2:3] * lv[3:4], axis=1, keepdims=True)) + lam_init)
    for h in range(DIFF_HEADS):
        acc = acc_ref[h]
        o = acc[:DIFF_V_DIM, :] / acc[DIFF_V_DIM:DIFF_V_DIM + 1, :]
        d = o[:, :tq] - lam * o[:, tq:]
        y = d * lax.rsqrt(jnp.mean(d * d, axis=0, keepdims=True) + EPS) * gsub_ref[...] * (1.0 - lam_init)
        o_ref[:, h * DIFF_V_DIM:(h + 1) * DIFF_V_DIM] = y.T.astype(BF16)


def _attn_a(lamv, q, k, vt, bias, gsub, *, batch, nq, tq, tkeys, q_tile_index, lam_init):
    tk = vt.shape[2]
    nkt = tkeys // tk
    kern = functools.partial(_attn_a_kernel, q_tile_index=q_tile_index, lam_init=lam_init)
    return pl.pallas_call(
        kern, grid=(batch, nq),
        in_specs=[_const_spec(lamv.shape),
                  pl.BlockSpec((tq, DIFF_QK), lambda b, i: (b * nq + i, 0)),
                  pl.BlockSpec((tkeys, DIFF_QK), lambda b, i: (b, 0)),
                  pl.BlockSpec((nkt, DIFF_HEADS * VA_ROWS, tk), lambda b, i: (b, 0, 0)),
                  _const_spec(bias.shape), _const_spec(gsub.shape)],
        out_specs=pl.BlockSpec((tq, DIFF_WIDTH), lambda b, i: (b * nq + i, 0)),
        out_shape=jax.ShapeDtypeStruct((batch * nq * tq, DIFF_WIDTH), BF16),
        scratch_shapes=[pltpu.VMEM((DIFF_HEADS, 1, 2 * tq), F32),
                        pltpu.VMEM((DIFF_HEADS, VA_ROWS, 2 * tq), F32)],
        compiler_params=_params(2), name="attn_a")(lamv, q, k, vt, bias, gsub)


def _attn_b_kernel(iq_ref, ikk_ref, iwt_ref, sq_ref, skk_ref, svt_ref, bias_ref, vis_ref, o_ref,
                   keys_ref, m_ref, acc_ref, *, q_tile_index, n_sel, idx_bits):
    tq = iq_ref.shape[0]
    tk = svt_ref.shape[2]
    i = pl.program_id(1) if q_tile_index is None else q_tile_index
    n_tiles = i + 1

    w_idx = _half_masked_weights(iq_ref[...], IDX_HEADS // 2)
    iw = iwt_ref[...]

    def score_keys(j, visible):
        row0 = pl.multiple_of(j * tk, tk)
        rel = _dot_nt(ikk_ref[pl.ds(row0, tk), :], w_idx)
        score = None
        for h in range(IDX_HEADS):
            term = iw[h:h + 1, :] * jnp.maximum(rel[:, h * tq:(h + 1) * tq], 0.0)
            score = term if score is None else score + term
        bits = pltpu.bitcast(score, I32)
        sign = bits >> 31
        key = (bits ^ (sign & 0x7FFFFFFF)) - sign
        if visible is not None:
            key = jnp.where(visible > 0.0, key, INT_MIN)
        keys_ref[pl.ds(row0, tk), :] = key

    def idx_body(j, carry):
        score_keys(j, None)
        return carry

    lax.fori_loop(0, i, idx_body, 0)
    score_keys(i, vis_ref[...])

    def count(pred):
        def body(j, c):
            row0 = pl.multiple_of(j * tk, tk)
            hit = pred(keys_ref[pl.ds(row0, tk), :], row0)
            return c + jnp.sum(hit.astype(F32), axis=0, keepdims=True)
        return lax.fori_loop(0, n_tiles, body, jnp.zeros((1, tq), F32))

    def bit_body(p, carry):
        t, cnt_t = carry
        cand = t + jnp.left_shift(jnp.int32(1), 31 - p)
        cnt = count(lambda tile, _: tile >= cand)
        ok = cnt >= n_sel
        return jnp.where(ok, cand, t), jnp.where(ok, cnt, cnt_t)

    t0 = jnp.full((1, tq), INT_MIN, I32)
    c0 = jnp.full((1, tq), 1.0, F32) * (n_tiles * tk).astype(F32) if q_tile_index is None else \
        jnp.full((1, tq), float(n_tiles * tk), F32)
    thr, cnt_thr = lax.fori_loop(0, 32, bit_body, (t0, c0))
    cut = (thr != INT_MIN) & (cnt_thr > n_sel)
    thr = jnp.maximum(thr, INT_MIN + 1)
    row_iota = lax.broadcasted_iota(I32, (tk, tq), 0)

    def tie_limit():
        above = count(lambda tile, _: tile > thr)
        need = jnp.where(cut, n_sel - above, jnp.float32(2 ** 30))

        def idx_bit(p, lim):
            cand = lim + jnp.left_shift(jnp.int32(1), idx_bits - 1 - p)
            cnt = count(lambda tile, row0: (tile == thr) & (row_iota + row0 < cand))
            return jnp.where(cnt < need, cand, lim)

        return lax.fori_loop(0, idx_bits, idx_bit, jnp.zeros((1, tq), I32))

    any_cut = jnp.max(cut.astype(I32)) > 0
    last = lax.cond(any_cut, tie_limit, lambda: jnp.full((1, tq), 2 ** 30, I32))

    w_q = _half_masked_weights(sq_ref[...], DSA_HEADS // 2)
    m_ref[...] = jnp.full(m_ref.shape, M_INIT, F32)
    acc_ref[...] = jnp.zeros(acc_ref.shape, F32)

    def step(j, bias_idx):
        row0 = pl.multiple_of(j * tk, tk)
        keys = keys_ref[pl.ds(row0, tk), :]
        sel = (keys > thr) | ((keys == thr) & (row_iota + row0 <= last))
        s_all = _dot_nt(skk_ref[pl.ds(row0, tk), :], w_q)
        parts = []
        for h in range(DSA_HEADS):
            s = s_all[:, h * tq:(h + 1) * tq]
            if bias_idx is not None:
                s = s + bias_ref[h, bias_idx]
            parts.append(jnp.where(sel, s, S_MASKED))
        m_new, alpha, p = _softmax_step(jnp.concatenate(parts, axis=1), m_ref[...])
        acc_ref[...] = alpha * acc_ref[...] + _dot(svt_ref[j], p)
        m_ref[...] = m_new

    def far_body(j, carry):
        step(j, None)
        return carry

    if q_tile_index is None:
        lax.fori_loop(0, jnp.maximum(i - 1, 0), far_body, 0)
        pl.when(i >= 1)(lambda: step(i - 1, 0))
    else:
        lax.fori_loop(0, max(i - 1, 0), far_body, 0)
        if i >= 1:
            step(i - 1, 0)
    step(i, 1)

    acc = acc_ref[...]
    o = acc[:HEAD_DIM, :] / acc[HEAD_DIM:HEAD_DIM + 1, :]
    for pr in range(DSA_HEADS // 2):
        pair = jnp.concatenate([o[:, 2 * pr * tq:(2 * pr + 1) * tq], o[:, (2 * pr + 1) * tq:(2 * pr + 2) * tq]],
                               axis=0)
        o_ref[:, pr * LANES:(pr + 1) * LANES] = pair.T.astype(BF16)


def _attn_b(p, bias, vis, *, batch, nq, tq, tkeys, q_tile_index, n_sel):
    tk = p["svt"].shape[2]
    nkt = tkeys // tk
    idx_bits = max(1, (tkeys - 1).bit_length())
    kern = functools.partial(_attn_b_kernel, q_tile_index=q_tile_index, n_sel=n_sel, idx_bits=idx_bits)
    qrow = lambda b, i: (b * nq + i, 0)
    krow = lambda b, i: (b, 0)
    return pl.pallas_call(
        kern, grid=(batch, nq),
        in_specs=[pl.BlockSpec((tq, IDX_HEADS * IDX_DIM), qrow),
                  pl.BlockSpec((tkeys, LANES), krow),
                  pl.BlockSpec((8, tq), lambda b, i: (0, b * nq + i)),
                  pl.BlockSpec((tq, DSA_WIDTH), qrow),
                  pl.BlockSpec((tkeys, LANES), krow),
                  pl.BlockSpec((nkt, VB_ROWS, tk), lambda b, i: (b, 0, 0)),
                  _const_spec(bias.shape), _const_spec(vis.shape)],
        out_specs=pl.BlockSpec((tq, DSA_WIDTH), qrow),
        out_shape=jax.ShapeDtypeStruct((batch * nq * tq, DSA_WIDTH), BF16),
        scratch_shapes=[pltpu.VMEM((tkeys, tq), I32),
                        pltpu.VMEM((1, DSA_HEADS * tq), F32),
                        pltpu.VMEM((VB_ROWS, DSA_HEADS * tq), F32)],
        compiler_params=_params(2), name="attn_b")(
            p["iq"], p["ikk"], p["iwt"], p["sq"], p["skk"], p["svt"], bias, vis)


def _t5_bucket(rel):
    half = NUM_BUCKETS // 2
    exact = half // 2
    n = jnp.abs(rel)
    logn = jnp.log(jnp.maximum(n, 1).astype(F32) / exact)
    large = exact + (logn / math.log(MAX_DISTANCE / exact) * (half - exact)).astype(I32)
    large = jnp.minimum(large, half - 1)
    return jnp.where(rel > 0, half, 0) + jnp.where(n < exact, n, large)


def _bias_tiles(rel_bias, tk, tq, q_valid, k_valid_diag):
    kk = jnp.arange(tk, dtype=I32)[:, None]
    qq = jnp.arange(tq, dtype=I32)[None, :]
    far = rel_bias[_t5_bucket(jnp.full((), -(2 * MAX_DISTANCE), I32))]
    tiles = []
    for d in (-tk, 0):
        b = rel_bias[_t5_bucket(kk + d - qq)] - far
        tiles.append(jnp.transpose(b, (2, 0, 1)))
    vis = ((kk // CHUNK) <= (qq // CHUNK)) & (kk < k_valid_diag) | (qq >= q_valid)
    bias = jnp.stack(tiles, axis=1).astype(F32)
    bias = bias.at[:, 1].set(jnp.where(vis[None], bias[:, 1], M_INIT))
    return bias, vis.astype(F32)


def _token_mix(p, rel_bias, lamv, gsub, *, batch, nq, tq, tkeys, q_tile_index, q_valid, k_valid_diag,
               n_sel, lam_init):
    bias, vis = _bias_tiles(rel_bias, KEY_TILE, tq, q_valid, k_valid_diag)
    gsub_b = jnp.broadcast_to(gsub.astype(F32)[:, None], (DIFF_V_DIM, tq))
    o_a = _attn_a(lamv, p["dq"], p["dk"], p["dvt"], bias[:DIFF_HEADS], gsub_b, batch=batch, nq=nq, tq=tq,
                  tkeys=tkeys, q_tile_index=q_tile_index, lam_init=lam_init)
    o_b = _attn_b(p, bias[DIFF_HEADS:], vis, batch=batch, nq=nq, tq=tq, tkeys=tkeys,
                  q_tile_index=q_tile_index, n_sel=n_sel)
    return o_a, o_b


def _pad_rows(a, rows):
    return jnp.pad(a, ((0, 0), (0, rows - a.shape[1])) + ((0, 0),) * (a.ndim - 2))


def kernel(x_prompt, x_sample, cache_diff_k, cache_diff_v, cache_dsa_k, cache_dsa_v, cache_idx_k, g_ffn1, w1_gate, w1_up, w1_down, g_mix, w_in, g_dq, g_dk, g_sq, g_sk, lam_q1, lam_k1, lam_q2, lam_k2, g_subln, w_out, g_ffn2, w2_gate, w2_up, w2_down, rel_bias):
    depth = g_ffn1.shape[0]
    assert depth == 1, "one layer"
    l = 0
    batch, seq, d_model = x_prompt.shape
    sbatch, sseq, _ = x_sample.shape
    past = cache_diff_k.shape[2]
    tk = KEY_TILE
    assert seq % Q_TILE == 0 and Q_TILE % CHUNK == 0 and Q_TILE == tk
    assert past % tk == 0 and past % CHUNK == 0 and sseq <= CHUNK and sseq % 8 == 0
    lam_init = 0.8 - 0.6 * math.exp(-0.3 * l)

    bf = lambda w: w.astype(BF16)
    row = lambda g: g.astype(F32)[None, :]
    w1 = (row(g_ffn1[l]), bf(w1_gate[l]), bf(w1_up[l]), bf(w1_down[l]))
    w2 = (row(g_ffn2[l]), bf(w2_gate[l]), bf(w2_up[l]), bf(w2_down[l]))
    w_in_p = jnp.pad(bf(w_in[l]), ((0, 0), (0, IN_COLS_PAD - IN_COLS)))
    wo_a, wo_b = bf(w_out[l][:DIFF_WIDTH]), bf(w_out[l][DIFF_WIDTH:])
    lamv = jnp.pad(jnp.stack([lam_q1[l], lam_k1[l], lam_q2[l], lam_k2[l]]).astype(F32), ((0, 4), (0, LANES - HEAD_DIM)))
    proj = functools.partial(_proj, g_mix=row(g_mix[l]), w_in=w_in_p, g_dq=g_dq[l], g_dk=g_dk[l], g_sq=g_sq[l],
                             g_sk=g_sk[l])

    xp1 = _ffn(x_prompt.reshape(batch * seq, d_model), *w1)
    pp = proj(xp1)
    oa_p, ob_p = _token_mix(pp, rel_bias, lamv, g_subln[l], batch=batch, nq=seq // Q_TILE, tq=Q_TILE, tkeys=seq,
                            q_tile_index=None, q_valid=Q_TILE, k_valid_diag=tk,
                            n_sel=min(TOPK_MAX, seq // 4), lam_init=lam_init)
    yp = _out_ffn(xp1, oa_p, ob_p, wo_a, wo_b, *w2).reshape(batch, seq, d_model)

    ns = sbatch * sseq
    xs1 = _ffn(x_sample.reshape(ns, d_model), *w1)
    ps = proj(xs1)
    tq_s = LANES
    keys_s = past + tk
    per_b = lambda a: a.reshape(sbatch, sseq, a.shape[-1])

    def q_rows(a):
        return _pad_rows(per_b(a), tq_s).reshape(sbatch * tq_s, a.shape[-1])

    def k_rows(cache, new):
        both = jnp.concatenate([cache.astype(BF16), per_b(new)], axis=1)
        return _pad_rows(both, keys_s).reshape(sbatch * keys_s, new.shape[-1])

    def v_tiles(cache, new_t, rows, heads):
        rv = rows - ONES_ROWS
        c = cache.astype(BF16).reshape(sbatch, past // tk, tk, heads, rv)
        c = jnp.transpose(c, (0, 1, 3, 4, 2))
        c = jnp.concatenate([c, jnp.ones((sbatch, past // tk, heads, ONES_ROWS, tk), BF16)], axis=3)
        c = c.reshape(sbatch, past // tk, heads * rows, tk)
        n = new_t.reshape(heads * rows, sbatch, sseq)
        n = jnp.pad(jnp.transpose(n, (1, 0, 2)), ((0, 0), (0, 0), (0, tk - sseq)))[:, None]
        return jnp.concatenate([c, n], axis=1).reshape(sbatch * (past // tk + 1), heads * rows, tk)

    dup = lambda c: jnp.concatenate([c, c], axis=-1)
    iwt_s = jnp.pad(ps["iwt"].reshape(8, sbatch, sseq), ((0, 0), (0, 0), (0, tq_s - sseq))).reshape(8, sbatch * tq_s)
    psa = {
        "dq": q_rows(ps["dq"]), "sq": q_rows(ps["sq"]), "iq": q_rows(ps["iq"]), "iwt": iwt_s,
        "dk": k_rows(cache_diff_k[l].reshape(sbatch, past, DIFF_QK), ps["dk"]),
        "skk": k_rows(dup(cache_dsa_k[l]), ps["skk"]),
        "ikk": k_rows(dup(cache_idx_k[l]), ps["ikk"]),
        "dvt": v_tiles(cache_diff_v[l].reshape(sbatch, past, DIFF_WIDTH), ps["dvt"], VA_ROWS, DIFF_HEADS),
        "svt": v_tiles(cache_dsa_v[l], ps["svt"], VB_ROWS, 1),
    }
    oa_s, ob_s = _token_mix(psa, rel_bias, lamv, g_subln[l], batch=sbatch, nq=1, tq=tq_s, tkeys=keys_s,
                            q_tile_index=past // tk, q_valid=sseq, k_valid_diag=sseq,
                            n_sel=min(TOPK_MAX, (past + sseq) // 4), lam_init=lam_init)
    take = lambda o: o.reshape(sbatch, tq_s, o.shape[-1])[:, :sseq].reshape(ns, o.shape[-1])
    ys = _out_ffn(xs1, take(oa_s), take(ob_s), wo_a, wo_b, *w2).reshape(sbatch, sseq, d_model)

    def caches(p, b, t):
        return (p["dk_c"].reshape(depth, b, t, DIFF_HEADS, 2, HEAD_DIM),
                p["dv_c"].reshape(depth, b, t, DIFF_HEADS, DIFF_V_DIM),
                p["sk_c"].reshape(depth, b, t, HEAD_DIM), p["sv_c"].reshape(depth, b, t, HEAD_DIM),
                p["ik_c"].reshape(depth, b, t, IDX_DIM))

    return (yp, ys) + caches(pp, batch, seq) + caches(ps, sbatch, sseq)
```

```python
import functools
import math

import jax
import jax.numpy as jnp
from jax import lax
from jax.experimental import pallas as pl
from jax.experimental.pallas import tpu as pltpu

F32 = jnp.float32
BF16 = jnp.bfloat16
I32 = jnp.int32

CHUNK = 64
HEAD_DIM = 64
DIFF_HEADS = 4
DIFF_V_DIM = 2 * HEAD_DIM
DSA_HEADS = 8
IDX_HEADS = 4
IDX_DIM = 64
TOPK_MAX = 256
NUM_BUCKETS = 32
MAX_DISTANCE = 128
EPS = 1e-6
IDX_SCALE = (IDX_HEADS ** -0.5) * (IDX_DIM ** -0.5)
QK_SCALE = HEAD_DIM ** -0.5
LOG2E = math.log2(math.e)

DIFF_QK = DIFF_HEADS * 2 * HEAD_DIM
DIFF_WIDTH = DIFF_HEADS * DIFF_V_DIM
DSA_WIDTH = DSA_HEADS * HEAD_DIM
IN_COLS = 3 * DIFF_QK + DSA_WIDTH + 2 * HEAD_DIM + IDX_HEADS * IDX_DIM + IDX_DIM + IDX_HEADS

LANES = 128
BF16_SUBLANES = 16
MXU_COLS = 256
VMEM_LIMIT_BYTES = 60000 * 1024

IN_COLS_PAD = -(-IN_COLS // MXU_COLS) * MXU_COLS
ONES_ROWS = BF16_SUBLANES
VA_ROWS = DIFF_V_DIM + ONES_ROWS
VB_ROWS = HEAD_DIM + ONES_ROWS

KEY_TILE = 256
Q_TILE = 256
M_INIT = -1e30
S_MASKED = -2e30
INT_MIN = -(2 ** 31)
KEY_NEG_INF = INT_MIN + 0x7FFFFF
COUNT_ROWS = 32

_C_DQ, _C_DK, _C_DV, _C_SQ = 0, DIFF_QK, 2 * DIFF_QK, 3 * DIFF_QK
_C_SKV = _C_SQ + DSA_WIDTH
_C_IQ = _C_SKV + 2 * HEAD_DIM
_C_IKW = _C_IQ + IDX_HEADS * IDX_DIM


def _params(n_axes):
    return pltpu.CompilerParams(dimension_semantics=("arbitrary",) * n_axes,
                                vmem_limit_bytes=VMEM_LIMIT_BYTES)


def _const_spec(shape):
    nd = len(shape)
    return pl.BlockSpec(shape, lambda *_: (0,) * nd, pipeline_mode=pl.Buffered(1))


def _rms(x, g):
    return x * lax.rsqrt(jnp.mean(x * x, axis=-1, keepdims=True) + EPS) * g


def _dot(a, b):
    return jnp.dot(a, b, preferred_element_type=F32)


def _dot_nt(a, b):
    return lax.dot_general(a, b, (((1,), (1,)), ((), ())), preferred_element_type=F32)


def _ff_chunks(d_ff):
    step = 4 * MXU_COLS
    return [(s, min(step, d_ff - s)) for s in range(0, d_ff, step)]


def _swiglu_half(x, g_ref, wg_ref, wu_ref, wd_ref):
    h = _rms(x, g_ref[...]).astype(BF16)
    acc = None
    for start, width in _ff_chunks(wg_ref.shape[1]):
        gate = _dot(h, wg_ref[:, start:start + width])
        up = _dot(h, wu_ref[:, start:start + width])
        act = (gate * (1.0 / (1.0 + jnp.exp(-gate))) * up).astype(BF16)
        part = _dot(act, wd_ref[start:start + width, :])
        acc = part if acc is None else acc + part
    return x + 0.5 * acc


def _ffn_kernel(x_ref, g_ref, wg_ref, wu_ref, wd_ref, o_ref):
    o_ref[...] = _swiglu_half(x_ref[...], g_ref, wg_ref, wu_ref, wd_ref)


def _out_ffn_kernel(x_ref, oa_ref, ob_ref, woa_ref, wob_ref, g_ref, wg_ref, wu_ref, wd_ref, o_ref):
    x = x_ref[...] + _dot(oa_ref[...], woa_ref[...]) + _dot(ob_ref[...], wob_ref[...])
    o_ref[...] = _swiglu_half(x, g_ref, wg_ref, wu_ref, wd_ref)


def _row_tile(n):
    for tm in (512, 256, 128, 64, 32, 16, 8):
        if n % tm == 0:
            return tm
    raise ValueError(f"row count {n} is not a multiple of 8")


def _ffn(x, g, wg, wu, wd):
    n, d = x.shape
    tm = _row_tile(n)
    row = pl.BlockSpec((tm, d), lambda i: (i, 0))
    return pl.pallas_call(
        _ffn_kernel, grid=(n // tm,),
        in_specs=[row, _const_spec(g.shape), _const_spec(wg.shape), _const_spec(wu.shape), _const_spec(wd.shape)],
        out_specs=row, out_shape=jax.ShapeDtypeStruct((n, d), F32),
        compiler_params=_params(1), name="ffn")(x, g, wg, wu, wd)


def _out_ffn(x, oa, ob, woa, wob, g, wg, wu, wd):
    n, d = x.shape
    tm = _row_tile(n)
    row = pl.BlockSpec((tm, d), lambda i: (i, 0))
    half = pl.BlockSpec((tm, oa.shape[1]), lambda i: (i, 0))
    consts = [woa, wob, g, wg, wu, wd]
    return pl.pallas_call(
        _out_ffn_kernel, grid=(n // tm,),
        in_specs=[row, half, half] + [_const_spec(c.shape) for c in consts],
        out_specs=row, out_shape=jax.ShapeDtypeStruct((n, d), F32),
        compiler_params=_params(1), name="out_ffn")(x, oa, ob, *consts)


def _dup_halves(t, first):
    lane = lax.broadcasted_iota(I32, t.shape, 1)
    rolled = pltpu.roll(t, HEAD_DIM, axis=1)
    if first:
        return jnp.where(lane < HEAD_DIM, t, rolled)
    return jnp.where(lane < HEAD_DIM, rolled, t)


def _proj_kernel(x_ref, g_ref, win_ref, gdq_ref, gdk_ref, gsq_ref, gsk_ref, g512_ref, g128_ref,
                 dkc_ref, dvc_ref, skc_ref, svc_ref, ikc_ref,
                 dq_ref, dkb_ref, dvt_ref, sq_ref, skk_ref, svt_ref, iq_ref, ikk_ref, iwt_ref):
    h = _rms(x_ref[...], g_ref[...]).astype(BF16)
    z = _dot(h, win_ref[...])
    n_sub, _, t_sub = dvt_ref.shape

    def headnorm(seg, ones_ref, gain):
        ss = _dot((seg * seg).astype(BF16), ones_ref[...])
        return seg * lax.rsqrt(ss * (1.0 / HEAD_DIM) + EPS) * gain

    dq_ref[...] = headnorm(z[:, _C_DQ:_C_DQ + DIFF_QK], g512_ref, gdq_ref[...]).astype(BF16)
    dk = headnorm(z[:, _C_DK:_C_DK + DIFF_QK], g512_ref, gdk_ref[...])
    dkc_ref[...] = dk
    dkb_ref[...] = dk.astype(BF16)
    sq_ref[...] = headnorm(z[:, _C_SQ:_C_SQ + DSA_WIDTH], g512_ref, gsq_ref[...]).astype(BF16)

    dv = z[:, _C_DV:_C_DV + DIFF_WIDTH]
    dvc_ref[...] = dv
    ones_a = jnp.ones((ONES_ROWS, t_sub), BF16)
    for s in range(n_sub):
        for hh in range(DIFF_HEADS):
            blk = dv[s * t_sub:(s + 1) * t_sub, hh * DIFF_V_DIM:(hh + 1) * DIFF_V_DIM]
            dvt_ref[s, hh * VA_ROWS:hh * VA_ROWS + DIFF_V_DIM, :] = blk.T.astype(BF16)
            dvt_ref[s, hh * VA_ROWS + DIFF_V_DIM:(hh + 1) * VA_ROWS, :] = ones_a

    skv = z[:, _C_SKV:_C_SKV + LANES]
    skn = headnorm(skv, g128_ref, gsk_ref[...])
    skk = _dup_halves(skn, first=True)
    svv = _dup_halves(skv, first=False)
    skc_ref[...] = skk[:, :HEAD_DIM]
    svc_ref[...] = svv[:, :HEAD_DIM]
    skk_ref[...] = skk.astype(BF16)
    for s in range(n_sub):
        svt_ref[s, 0:HEAD_DIM, :] = svv[s * t_sub:(s + 1) * t_sub, :].T[:HEAD_DIM, :].astype(BF16)
        svt_ref[s, HEAD_DIM:VB_ROWS, :] = ones_a

    iq_ref[...] = z[:, _C_IQ:_C_IQ + IDX_HEADS * IDX_DIM].astype(BF16)
    ikw = z[:, _C_IKW:_C_IKW + LANES]
    ikk = _dup_halves(ikw, first=True)
    ikc_ref[...] = ikk[:, :HEAD_DIM]
    ikk_ref[...] = ikk.astype(BF16)
    iwt_ref[...] = ikw.T[HEAD_DIM:HEAD_DIM + 8, :] * IDX_SCALE


def _proj(x1, g_mix, w_in, g_dq, g_dk, g_sq, g_sk):
    n, d = x1.shape
    tm = _row_tile(n)
    t_sub = min(KEY_TILE, tm)
    n_sub = tm // t_sub
    nt = n // t_sub

    def tile8(g, scale):
        return (jnp.tile(g.astype(F32), DIFF_QK // HEAD_DIM) * scale)[None, :]

    gsk_pad = jnp.concatenate([g_sk.astype(F32), jnp.ones((HEAD_DIM,), F32)])[None, :]
    grp = jnp.arange(DIFF_QK) // HEAD_DIM
    ones512 = (grp[:, None] == grp[None, :]).astype(BF16)
    ones128 = ones512[:LANES, :LANES]
    consts = [g_mix, w_in, tile8(g_dq, QK_SCALE * LOG2E), tile8(g_dk, 1.0), tile8(g_sq, QK_SCALE * LOG2E), gsk_pad,
              ones512, ones128]

    def rows(width):
        return pl.BlockSpec((tm, width), lambda i: (i, 0))

    def tposed(nrows):
        return pl.BlockSpec((n_sub, nrows, t_sub), lambda i: (i, 0, 0))

    out_shapes = [
        jax.ShapeDtypeStruct((n, DIFF_QK), F32), jax.ShapeDtypeStruct((n, DIFF_WIDTH), F32),
        jax.ShapeDtypeStruct((n, HEAD_DIM), F32), jax.ShapeDtypeStruct((n, HEAD_DIM), F32),
        jax.ShapeDtypeStruct((n, IDX_DIM), F32),
        jax.ShapeDtypeStruct((n, DIFF_QK), BF16), jax.ShapeDtypeStruct((n, DIFF_QK), BF16),
        jax.ShapeDtypeStruct((nt, DIFF_HEADS * VA_ROWS, t_sub), BF16),
        jax.ShapeDtypeStruct((n, DSA_WIDTH), BF16), jax.ShapeDtypeStruct((n, LANES), BF16),
        jax.ShapeDtypeStruct((nt, VB_ROWS, t_sub), BF16),
        jax.ShapeDtypeStruct((n, IDX_HEADS * IDX_DIM), BF16), jax.ShapeDtypeStruct((n, LANES), BF16),
        jax.ShapeDtypeStruct((8, n), F32),
    ]
    out_specs = [rows(DIFF_QK), rows(DIFF_WIDTH), rows(HEAD_DIM), rows(HEAD_DIM), rows(IDX_DIM),
                 rows(DIFF_QK), rows(DIFF_QK), tposed(DIFF_HEADS * VA_ROWS),
                 rows(DSA_WIDTH), rows(LANES), tposed(VB_ROWS),
                 rows(IDX_HEADS * IDX_DIM), rows(LANES), pl.BlockSpec((8, tm), lambda i: (0, i))]
    outs = pl.pallas_call(
        _proj_kernel, grid=(n // tm,),
        in_specs=[rows(d)] + [_const_spec(c.shape) for c in consts],
        out_specs=out_specs, out_shape=out_shapes,
        compiler_params=_params(1), name="proj")(x1, *consts)
    names = ("dk_c", "dv_c", "sk_c", "sv_c", "ik_c", "dq", "dk", "dvt", "sq", "skk", "svt", "iq", "ikk", "iwt")
    return dict(zip(names, outs))


def _half_masked_weights(q_tile, n_pairs):
    lane = lax.broadcasted_iota(I32, (q_tile.shape[0], LANES), 1)
    parts = []
    for pr in range(n_pairs):
        qp = q_tile[:, pr * LANES:(pr + 1) * LANES].astype(F32)
        parts.append(jnp.where(lane < HEAD_DIM, qp, 0.0).astype(q_tile.dtype))
        parts.append(jnp.where(lane >= HEAD_DIM, qp, 0.0).astype(q_tile.dtype))
    return jnp.concatenate(parts, axis=0)


def _softmax_step(s, m_old):
    m_new = jnp.maximum(m_old, jnp.max(s, axis=0, keepdims=True))
    alpha = jnp.exp2(m_old - m_new)
    p = jnp.exp2(s - m_new).astype(BF16)
    return m_new, alpha, p


def _run_tiles(i, step):
    def far_pair(p, carry):
        step(2 * p, 2, False)
        return carry

    if isinstance(i, int):
        n_far = max(i - 1, 0)
        lax.fori_loop(0, n_far // 2, far_pair, 0)
        if n_far % 2:
            step(n_far - 1, 1, False)
        if i >= 1:
            step(i - 1, 2, True)
        else:
            step(0, 1, True)
    else:
        n_far = jnp.maximum(i - 1, 0)
        lax.fori_loop(0, lax.shift_right_logical(n_far, 1), far_pair, 0)
        pl.when((n_far & 1) == 1)(lambda: step(n_far - 1, 1, False))
        pl.when(i >= 1)(lambda: step(i - 1, 2, True))
        pl.when(i == 0)(lambda: step(0, 1, True))


def _row_start(j, tk):
    return j * tk if isinstance(j, int) else pl.multiple_of(j * tk, tk)


def _attn_a_kernel(lamv_ref, q_ref, k_ref, vt_ref, bias_ref, gsub_ref, o_ref, m_ref, acc_ref,
                   *, q_tile_index, lam_init):
    tq = q_ref.shape[0]
    tk = vt_ref.shape[2]
    i = pl.program_id(1) if q_tile_index is None else q_tile_index
    weights = _half_masked_weights(q_ref[...], DIFF_HEADS)
    m_ref[...] = jnp.full(m_ref.shape, M_INIT, F32)
    acc_ref[...] = jnp.zeros(acc_ref.shape, F32)

    def step(j0, width, near):
        row0 = _row_start(j0, tk)
        rows = width * tk
        scores = [_dot_nt(k_ref[pl.ds(row0, rows), h * LANES:(h + 1) * LANES],
                          weights[2 * h * tq:(2 * h + 2) * tq, :])
                  for h in range(DIFF_HEADS)]
        for h in range(DIFF_HEADS):
            s = scores[h]
            if near:
                b = bias_ref[h] if width == 2 else bias_ref[h, tk:, :]
                s = s + jnp.concatenate([b, b], axis=1)
            m_new, alpha, p = _softmax_step(s, m_ref[h])
            m_ref[h] = m_new
            vt = jnp.concatenate([vt_ref[j0 + w, h * VA_ROWS:(h + 1) * VA_ROWS, :] for w in range(width)], axis=1)
            acc_ref[h] = alpha * acc_ref[h] + _dot(vt, p)

    _run_tiles(i, step)

    lv = lamv_ref[...]
    lam = (jnp.exp(jnp.sum(lv[0:1] * lv[1:2], axis=1, keepdims=True))
           - jnp.exp(jnp.sum(lv[2:3] * lv[3:4], axis=1, keepdims=True)) + lam_init)
    for h in range(DIFF_HEADS):
        acc = acc_ref[h]
        o = acc[:DIFF_V_DIM, :] / acc[DIFF_V_DIM:DIFF_V_DIM + 1, :]
        d = o[:, :tq] - lam * o[:, tq:]
        y = d * lax.rsqrt(jnp.mean(d * d, axis=0, keepdims=True) + EPS) * gsub_ref[...] * (1.0 - lam_init)
        o_ref[:, h * DIFF_V_DIM:(h + 1) * DIFF_V_DIM] = y.T.astype(BF16)


def _attn_a(lamv, q, k, vt, bias, gsub, *, batch, nq, tq, tkeys, q_tile_index, lam_init):
    tk = vt.shape[2]
    nkt = tkeys // tk
    kern = functools.partial(_attn_a_kernel, q_tile_index=q_tile_index, lam_init=lam_init)
    return pl.pallas_call(
        kern, grid=(batch, nq),
        in_specs=[_const_spec(lamv.shape),
                  pl.BlockSpec((tq, DIFF_QK), lambda b, i: (b * nq + i, 0)),
                  pl.BlockSpec((tkeys, DIFF_QK), lambda b, i: (b, 0)),
                  pl.BlockSpec((nkt, DIFF_HEADS * VA_ROWS, tk), lambda b, i: (b, 0, 0)),
                  _const_spec(bias.shape), _const_spec(gsub.shape)],
        out_specs=pl.BlockSpec((tq, DIFF_WIDTH), lambda b, i: (b * nq + i, 0)),
        out_shape=jax.ShapeDtypeStruct((batch * nq * tq, DIFF_WIDTH), BF16),
        scratch_shapes=[pltpu.VMEM((DIFF_HEADS, 1, 2 * tq), F32),
                        pltpu.VMEM((DIFF_HEADS, VA_ROWS, 2 * tq), F32)],
        compiler_params=_params(2), name="attn_a")(lamv, q, k, vt, bias, gsub)


def _key_to_f32(key):
    return pltpu.bitcast(key ^ ((key >> 31) & 0x7FFFFFFF), F32)


def _attn_b_kernel(iq_ref, ikk_ref, iwt_ref, sq_ref, skk_ref, svt_ref, bias_ref, vis_ref, o_ref,
                   sc_ref, m_ref, acc_ref, *, q_tile_index, n_sel, idx_bits):
    tq = iq_ref.shape[0]
    tk = svt_ref.shape[2]
    static = q_tile_index is not None
    i = q_tile_index if static else pl.program_id(1)
    n_tiles = i + 1
    n_pairs = (n_tiles + 1) // 2 if static else lax.shift_right_logical(n_tiles + 1, 1)
    neg_inf = float("-inf")

    w_idx = _half_masked_weights(iq_ref[...], IDX_HEADS // 2)
    iw = iwt_ref[...]

    def index_scores(j, visible):
        row0 = _row_start(j, tk)
        rel = _dot_nt(ikk_ref[pl.ds(row0, tk), :], w_idx)
        score = None
        for h in range(IDX_HEADS):
            term = iw[h:h + 1, :] * jnp.maximum(rel[:, h * tq:(h + 1) * tq], 0.0)
            score = term if score is None else score + term
        if visible is not None:
            score = jnp.where(visible > 0.0, score, neg_inf)
        sc_ref[pl.ds(row0, tk), :] = score

    def idx_body(j, carry):
        index_scores(j, None)
        return carry

    lax.fori_loop(0, i, idx_body, 0)
    index_scores(i, vis_ref[...])
    sc_ref[pl.ds(_row_start(n_tiles, tk), tk), :] = jnp.full((tk, tq), neg_inf, F32)

    pair_iota = lax.broadcasted_iota(I32, (2 * tk, tq), 0)

    def count(pred):
        def body(pp, acc):
            row0 = pl.multiple_of(pp * (2 * tk), 2 * tk)
            hit = pred(sc_ref[pl.ds(row0, 2 * tk), :], row0).astype(F32)
            return acc + jnp.sum(hit.reshape(2 * tk // COUNT_ROWS, COUNT_ROWS, tq), axis=0)
        acc = lax.fori_loop(0, n_pairs, body, jnp.zeros((COUNT_ROWS, tq), F32))
        return jnp.sum(acc, axis=0, keepdims=True)

    def bit_body(p, carry):
        t, cnt_t = carry
        cand = t + jnp.left_shift(jnp.int32(1), 31 - p)
        cand_f = _key_to_f32(cand)
        cnt = count(lambda tile, _: tile >= cand_f)
        ok = (cnt >= n_sel) | (cand < KEY_NEG_INF)
        return jnp.where(ok, cand, t), jnp.where(ok, cnt, cnt_t)

    t0 = jnp.full((1, tq), INT_MIN, I32)
    thr_key, cnt_thr = lax.fori_loop(0, 32, bit_body, (t0, jnp.zeros((1, tq), F32)))
    thr = _key_to_f32(thr_key)
    cut = (thr > neg_inf) & (cnt_thr > n_sel)
    thr = jnp.maximum(thr, jnp.finfo(F32).min)

    def tie_limit():
        above = count(lambda tile, _: tile > thr)
        need = jnp.where(cut, n_sel - above, jnp.float32(2 ** 30))

        def idx_bit(p, lim):
            cand = lim + jnp.left_shift(jnp.int32(1), idx_bits - 1 - p)
            cnt = count(lambda tile, row0: (tile == thr) & (pair_iota + row0 < cand))
            return jnp.where(cnt < need, cand, lim)

        return lax.fori_loop(0, idx_bits, idx_bit, jnp.zeros((1, tq), I32))

    any_cut = jnp.max(cut.astype(I32)) > 0
    last = lax.cond(any_cut, tie_limit, lambda: jnp.full((1, tq), 2 ** 30, I32))

    w_q = _half_masked_weights(sq_ref[...], DSA_HEADS // 2)
    m_ref[...] = jnp.full(m_ref.shape, M_INIT, F32)
    acc_ref[...] = jnp.zeros(acc_ref.shape, F32)

    def step(j0, width, near):
        row0 = _row_start(j0, tk)
        rows = width * tk
        s_all = _dot_nt(skk_ref[pl.ds(row0, rows), :], w_q)
        sc = sc_ref[pl.ds(row0, rows), :]
        key_index = lax.broadcasted_iota(I32, (rows, tq), 0) + row0
        sel = (sc > thr) | ((sc == thr) & (key_index <= last))
        parts = []
        for h in range(DSA_HEADS):
            s = s_all[:, h * tq:(h + 1) * tq]
            if near:
                s = s + (bias_ref[h] if width == 2 else bias_ref[h, tk:, :])
            parts.append(jnp.where(sel, s, S_MASKED))
        m_new, alpha, p = _softmax_step(jnp.concatenate(parts, axis=1), m_ref[...])
        vt = jnp.concatenate([svt_ref[j0 + w] for w in range(width)], axis=1)
        acc_ref[...] = alpha * acc_ref[...] + _dot(vt, p)
        m_ref[...] = m_new

    _run_tiles(i, step)

    acc = acc_ref[...]
    o = acc[:HEAD_DIM, :] / acc[HEAD_DIM:HEAD_DIM + 1, :]
    for pr in range(DSA_HEADS // 2):
        pair = jnp.concatenate([o[:, 2 * pr * tq:(2 * pr + 1) * tq], o[:, (2 * pr + 1) * tq:(2 * pr + 2) * tq]],
                               axis=0)
        o_ref[:, pr * LANES:(pr + 1) * LANES] = pair.T.astype(BF16)


def _attn_b(p, bias, vis, *, batch, nq, tq, tkeys, q_tile_index, n_sel):
    tk = p["svt"].shape[2]
    nkt = tkeys // tk
    idx_bits = max(1, (tkeys + tk - 1).bit_length())
    kern = functools.partial(_attn_b_kernel, q_tile_index=q_tile_index, n_sel=n_sel, idx_bits=idx_bits)
    qrow = lambda b, i: (b * nq + i, 0)
    krow = lambda b, i: (b, 0)
    return pl.pallas_call(
        kern, grid=(batch, nq),
        in_specs=[pl.BlockSpec((tq, IDX_HEADS * IDX_DIM), qrow),
                  pl.BlockSpec((tkeys, LANES), krow),
                  pl.BlockSpec((8, tq), lambda b, i: (0, b * nq + i)),
                  pl.BlockSpec((tq, DSA_WIDTH), qrow),
                  pl.BlockSpec((tkeys, LANES), krow),
                  pl.BlockSpec((nkt, VB_ROWS, tk), lambda b, i: (b, 0, 0)),
                  _const_spec(bias.shape), _const_spec(vis.shape)],
        out_specs=pl.BlockSpec((tq, DSA_WIDTH), qrow),
        out_shape=jax.ShapeDtypeStruct((batch * nq * tq, DSA_WIDTH), BF16),
        scratch_shapes=[pltpu.VMEM((tkeys + tk, tq), F32),
                        pltpu.VMEM((1, DSA_HEADS * tq), F32),
                        pltpu.VMEM((VB_ROWS, DSA_HEADS * tq), F32)],
        compiler_params=_params(2), name="attn_b")(
            p["iq"], p["ikk"], p["iwt"], p["sq"], p["skk"], p["svt"], bias, vis)


def _t5_bucket(rel):
    half = NUM_BUCKETS // 2
    exact = half // 2
    n = jnp.abs(rel)
    logn = jnp.log(jnp.maximum(n, 1).astype(F32) / exact)
    large = exact + (logn / math.log(MAX_DISTANCE / exact) * (half - exact)).astype(I32)
    large = jnp.minimum(large, half - 1)
    return jnp.where(rel > 0, half, 0) + jnp.where(n < exact, n, large)


def _bias_lookup(rel_bias, bucket):
    out = jnp.zeros((rel_bias.shape[1],) + bucket.shape, F32)
    for b in range(NUM_BUCKETS):
        out = out + jnp.where(bucket[None] == b, rel_bias[b].astype(F32)[:, None, None], 0.0)
    return out


def _bias_tiles(rel_bias, tk, tq, q_valid, k_valid_diag):
    kk = jnp.arange(tk, dtype=I32)[:, None]
    qq = jnp.arange(tq, dtype=I32)[None, :]
    far = _bias_lookup(rel_bias, _t5_bucket(jnp.full((1, 1), -(2 * MAX_DISTANCE), I32)))
    tiles = [(_bias_lookup(rel_bias, _t5_bucket(kk + d - qq)) - far) * LOG2E for d in (-tk, 0)]
    vis = ((kk // CHUNK) <= (qq // CHUNK)) & (kk < k_valid_diag) | (qq >= q_valid)
    tiles[1] = jnp.where(vis[None], tiles[1], M_INIT)
    return jnp.concatenate(tiles, axis=1), vis.astype(F32)


def _token_mix(p, rel_bias, lamv, gsub, *, batch, nq, tq, tkeys, q_tile_index, q_valid, k_valid_diag,
               n_sel, lam_init):
    bias, vis = _bias_tiles(rel_bias, KEY_TILE, tq, q_valid, k_valid_diag)
    gsub_b = jnp.broadcast_to(gsub.astype(F32)[:, None], (DIFF_V_DIM, tq))
    o_a = _attn_a(lamv, p["dq"], p["dk"], p["dvt"], bias[:DIFF_HEADS], gsub_b, batch=batch, nq=nq, tq=tq,
                  tkeys=tkeys, q_tile_index=q_tile_index, lam_init=lam_init)
    o_b = _attn_b(p, bias[DIFF_HEADS:], vis, batch=batch, nq=nq, tq=tq, tkeys=tkeys,
                  q_tile_index=q_tile_index, n_sel=n_sel)
    return o_a, o_b


def _pad_rows(a, rows):
    return jnp.pad(a, ((0, 0), (0, rows - a.shape[1])) + ((0, 0),) * (a.ndim - 2))


def kernel(x_prompt, x_sample, cache_diff_k, cache_diff_v, cache_dsa_k, cache_dsa_v, cache_idx_k, g_ffn1, w1_gate, w1_up, w1_down, g_mix, w_in, g_dq, g_dk, g_sq, g_sk, lam_q1, lam_k1, lam_q2, lam_k2, g_subln, w_out, g_ffn2, w2_gate, w2_up, w2_down, rel_bias):
    depth = g_ffn1.shape[0]
    assert depth == 1, "one layer"
    l = 0
    batch, seq, d_model = x_prompt.shape
    sbatch, sseq, _ = x_sample.shape
    past = cache_diff_k.shape[2]
    tk = KEY_TILE
    assert seq % Q_TILE == 0 and Q_TILE % CHUNK == 0 and Q_TILE == tk
    assert past % tk == 0 and past % CHUNK == 0 and sseq <= CHUNK and sseq % 8 == 0
    lam_init = 0.8 - 0.6 * math.exp(-0.3 * l)

    bf = lambda w: w.astype(BF16)
    row = lambda g: g.astype(F32)[None, :]
    w1 = (row(g_ffn1[l]), bf(w1_gate[l]), bf(w1_up[l]), bf(w1_down[l]))
    w2 = (row(g_ffn2[l]), bf(w2_gate[l]), bf(w2_up[l]), bf(w2_down[l]))
    w_in_p = jnp.pad(bf(w_in[l]), ((0, 0), (0, IN_COLS_PAD - IN_COLS)))
    wo_a, wo_b = bf(w_out[l][:DIFF_WIDTH]), bf(w_out[l][DIFF_WIDTH:])
    lamv = jnp.pad(jnp.stack([lam_q1[l], lam_k1[l], lam_q2[l], lam_k2[l]]).astype(F32), ((0, 4), (0, LANES - HEAD_DIM)))
    proj = functools.partial(_proj, g_mix=row(g_mix[l]), w_in=w_in_p, g_dq=g_dq[l], g_dk=g_dk[l], g_sq=g_sq[l],
                             g_sk=g_sk[l])

    xp1 = _ffn(x_prompt.reshape(batch * seq, d_model), *w1)
    pp = proj(xp1)
    oa_p, ob_p = _token_mix(pp, rel_bias, lamv, g_subln[l], batch=batch, nq=seq // Q_TILE, tq=Q_TILE, tkeys=seq,
                            q_tile_index=None, q_valid=Q_TILE, k_valid_diag=tk,
                            n_sel=min(TOPK_MAX, seq // 4), lam_init=lam_init)
    yp = _out_ffn(xp1, oa_p, ob_p, wo_a, wo_b, *w2).reshape(batch, seq, d_model)

    ns = sbatch * sseq
    xs1 = _ffn(x_sample.reshape(ns, d_model), *w1)
    ps = proj(xs1)
    tq_s = LANES
    keys_s = past + tk
    per_b = lambda a: a.reshape(sbatch, sseq, a.shape[-1])

    def q_rows(a):
        return _pad_rows(per_b(a), tq_s).reshape(sbatch * tq_s, a.shape[-1])

    def k_rows(cache, new):
        both = jnp.concatenate([cache.astype(BF16), per_b(new)], axis=1)
        return _pad_rows(both, keys_s).reshape(sbatch * keys_s, new.shape[-1])

    def v_tiles(cache, new_t, rows, heads):
        rv = rows - ONES_ROWS
        c = cache.astype(BF16).reshape(sbatch, past // tk, tk, heads, rv)
        c = jnp.transpose(c, (0, 1, 3, 4, 2))
        c = jnp.concatenate([c, jnp.ones((sbatch, past // tk, heads, ONES_ROWS, tk), BF16)], axis=3)
        c = c.reshape(sbatch, past // tk, heads * rows, tk)
        n = new_t.reshape(heads * rows, sbatch, sseq)
        n = jnp.pad(jnp.transpose(n, (1, 0, 2)), ((0, 0), (0, 0), (0, tk - sseq)))[:, None]
        return jnp.concatenate([c, n], axis=1).reshape(sbatch * (past // tk + 1), heads * rows, tk)

    dup = lambda c: jnp.concatenate([c, c], axis=-1)
    iwt_s = jnp.pad(ps["iwt"].reshape(8, sbatch, sseq), ((0, 0), (0, 0), (0, tq_s - sseq))).reshape(8, sbatch * tq_s)
    psa = {
        "dq": q_rows(ps["dq"]), "sq": q_rows(ps["sq"]), "iq": q_rows(ps["iq"]), "iwt": iwt_s,
        "dk": k_rows(cache_diff_k[l].reshape(sbatch, past, DIFF_QK), ps["dk"]),
        "skk": k_rows(dup(cache_dsa_k[l]), ps["skk"]),
        "ikk": k_rows(dup(cache_idx_k[l]), ps["ikk"]),
        "dvt": v_tiles(cache_diff_v[l].reshape(sbatch, past, DIFF_WIDTH), ps["dvt"], VA_ROWS, DIFF_HEADS),
        "svt": v_tiles(cache_dsa_v[l], ps["svt"], VB_ROWS, 1),
    }
    oa_s, ob_s = _token_mix(psa, rel_bias, lamv, g_subln[l], batch=sbatch, nq=1, tq=tq_s, tkeys=keys_s,
                            q_tile_index=past // tk, q_valid=sseq, k_valid_diag=sseq,
                            n_sel=min(TOPK_MAX, (past + sseq) // 4), lam_init=lam_init)
    take = lambda o: o.reshape(sbatch, tq_s, o.shape[-1])[:, :sseq].reshape(ns, o.shape[-1])
    ys = _out_ffn(xs1, take(oa_s), take(ob_s), wo_a, wo_b, *w2).reshape(sbatch, sseq, d_model)

    def caches(p, b, t):
        return (p["dk_c"].reshape(depth, b, t, DIFF_HEADS, 2, HEAD_DIM),
                p["dv_c"].reshape(depth, b, t, DIFF_HEADS, DIFF_V_DIM),
                p["sk_c"].reshape(depth, b, t, HEAD_DIM), p["sv_c"].reshape(depth, b, t, HEAD_DIM),
                p["ik_c"].reshape(depth, b, t, IDX_DIM))

    return (yp, ys) + caches(pp, batch, seq) + caches(ps, sbatch, sseq)
```

```python
import functools
import math

import jax
import jax.numpy as jnp
from jax import lax
from jax.experimental import pallas as pl
from jax.experimental.pallas import tpu as pltpu

F32 = jnp.float32
BF16 = jnp.bfloat16
I32 = jnp.int32

CHUNK = 64
HEAD_DIM = 64
DIFF_HEADS = 4
DIFF_V_DIM = 2 * HEAD_DIM
DSA_HEADS = 8
IDX_HEADS = 4
IDX_DIM = 64
TOPK_MAX = 256
NUM_BUCKETS = 32
MAX_DISTANCE = 128
EPS = 1e-6
IDX_SCALE = (IDX_HEADS ** -0.5) * (IDX_DIM ** -0.5)
QK_SCALE = HEAD_DIM ** -0.5
LOG2E = math.log2(math.e)

DIFF_QK = DIFF_HEADS * 2 * HEAD_DIM
DIFF_WIDTH = DIFF_HEADS * DIFF_V_DIM
DSA_WIDTH = DSA_HEADS * HEAD_DIM
IN_COLS = 3 * DIFF_QK + DSA_WIDTH + 2 * HEAD_DIM + IDX_HEADS * IDX_DIM + IDX_DIM + IDX_HEADS

LANES = 128
BF16_SUBLANES = 16
MXU_COLS = 256
VMEM_LIMIT_BYTES = 60000 * 1024

IN_COLS_PAD = -(-IN_COLS // MXU_COLS) * MXU_COLS
ONES_ROWS = BF16_SUBLANES
VA_ROWS = DIFF_V_DIM + ONES_ROWS
VB_ROWS = HEAD_DIM + ONES_ROWS

KEY_TILE = 256
Q_TILE = 256
M_INIT = -1e30
S_MASKED = -2e30
INT_MIN = -(2 ** 31)
KEY_NEG_INF = INT_MIN + 0x7FFFFF
COUNT_ROWS = 32

_C_DQ, _C_DK, _C_DV, _C_SQ = 0, DIFF_QK, 2 * DIFF_QK, 3 * DIFF_QK
_C_SKV = _C_SQ + DSA_WIDTH
_C_IQ = _C_SKV + 2 * HEAD_DIM
_C_IKW = _C_IQ + IDX_HEADS * IDX_DIM


def _params(n_axes):
    return pltpu.CompilerParams(dimension_semantics=("arbitrary",) * n_axes,
                                vmem_limit_bytes=VMEM_LIMIT_BYTES)


def _const_spec(shape):
    nd = len(shape)
    return pl.BlockSpec(shape, lambda *_: (0,) * nd, pipeline_mode=pl.Buffered(1))


def _rms(x, g):
    return x * lax.rsqrt(jnp.mean(x * x, axis=-1, keepdims=True) + EPS) * g


def _dot(a, b):
    return jnp.dot(a, b, preferred_element_type=F32)


def _dot_nt(a, b):
    return lax.dot_general(a, b, (((1,), (1,)), ((), ())), preferred_element_type=F32)


def _ff_chunks(d_ff):
    step = 4 * MXU_COLS
    return [(s, min(step, d_ff - s)) for s in range(0, d_ff, step)]


def _swiglu_half(x, g_ref, wg_ref, wu_ref, wd_ref):
    h = _rms(x, g_ref[...]).astype(BF16)
    acc = None
    for start, width in _ff_chunks(wg_ref.shape[1]):
        gate = _dot(h, wg_ref[:, start:start + width])
        up = _dot(h, wu_ref[:, start:start + width])
        act = (gate * (1.0 / (1.0 + jnp.exp(-gate))) * up).astype(BF16)
        part = _dot(act, wd_ref[start:start + width, :])
        acc = part if acc is None else acc + part
    return x + 0.5 * acc


def _ffn_kernel(x_ref, g_ref, wg_ref, wu_ref, wd_ref, o_ref):
    o_ref[...] = _swiglu_half(x_ref[...], g_ref, wg_ref, wu_ref, wd_ref)


def _out_ffn_kernel(x_ref, oa_ref, ob_ref, woa_ref, wob_ref, g_ref, wg_ref, wu_ref, wd_ref, o_ref):
    x = x_ref[...] + _dot(oa_ref[...], woa_ref[...]) + _dot(ob_ref[...], wob_ref[...])
    o_ref[...] = _swiglu_half(x, g_ref, wg_ref, wu_ref, wd_ref)


def _row_tile(n):
    for tm in (512, 256, 128, 64, 32, 16, 8):
        if n % tm == 0:
            return tm
    raise ValueError(f"row count {n} is not a multiple of 8")


def _ffn(x, g, wg, wu, wd):
    n, d = x.shape
    tm = _row_tile(n)
    row = pl.BlockSpec((tm, d), lambda i: (i, 0))
    return pl.pallas_call(
        _ffn_kernel, grid=(n // tm,),
        in_specs=[row, _const_spec(g.shape), _const_spec(wg.shape), _const_spec(wu.shape), _const_spec(wd.shape)],
        out_specs=row, out_shape=jax.ShapeDtypeStruct((n, d), F32),
        compiler_params=_params(1), name="ffn")(x, g, wg, wu, wd)


def _out_ffn(x, oa, ob, woa, wob, g, wg, wu, wd):
    n, d = x.shape
    tm = _row_tile(n)
    row = pl.BlockSpec((tm, d), lambda i: (i, 0))
    half = pl.BlockSpec((tm, oa.shape[1]), lambda i: (i, 0))
    consts = [woa, wob, g, wg, wu, wd]
    return pl.pallas_call(
        _out_ffn_kernel, grid=(n // tm,),
        in_specs=[row, half, half] + [_const_spec(c.shape) for c in consts],
        out_specs=row, out_shape=jax.ShapeDtypeStruct((n, d), F32),
        compiler_params=_params(1), name="out_ffn")(x, oa, ob, *consts)


def _dup_halves(t, first):
    lane = lax.broadcasted_iota(I32, t.shape, 1)
    rolled = pltpu.roll(t, HEAD_DIM, axis=1)
    if first:
        return jnp.where(lane < HEAD_DIM, t, rolled)
    return jnp.where(lane < HEAD_DIM, rolled, t)


def _proj_kernel(x_ref, g_ref, win_ref, gdq_ref, gdk_ref, gsq_ref, gsk_ref, g512_ref, g128_ref,
                 dkc_ref, dvc_ref, skc_ref, svc_ref, ikc_ref,
                 dq_ref, dkb_ref, dvt_ref, sq_ref, skk_ref, svt_ref, iq_ref, ikk_ref, iwt_ref):
    h = _rms(x_ref[...], g_ref[...]).astype(BF16)
    z = _dot(h, win_ref[...])
    n_sub, _, t_sub = dvt_ref.shape

    def headnorm(seg, ones_ref, gain):
        ss = _dot((seg * seg).astype(BF16), ones_ref[...])
        return seg * lax.rsqrt(ss * (1.0 / HEAD_DIM) + EPS) * gain

    dq_ref[...] = headnorm(z[:, _C_DQ:_C_DQ + DIFF_QK], g512_ref, gdq_ref[...]).astype(BF16)
    dk = headnorm(z[:, _C_DK:_C_DK + DIFF_QK], g512_ref, gdk_ref[...])
    dkc_ref[...] = dk
    dkb_ref[...] = dk.astype(BF16)
    sq_ref[...] = headnorm(z[:, _C_SQ:_C_SQ + DSA_WIDTH], g512_ref, gsq_ref[...]).astype(BF16)

    dv = z[:, _C_DV:_C_DV + DIFF_WIDTH]
    dvc_ref[...] = dv
    ones_a = jnp.ones((ONES_ROWS, t_sub), BF16)
    for s in range(n_sub):
        for hh in range(DIFF_HEADS):
            blk = dv[s * t_sub:(s + 1) * t_sub, hh * DIFF_V_DIM:(hh + 1) * DIFF_V_DIM]
            dvt_ref[s, hh * VA_ROWS:hh * VA_ROWS + DIFF_V_DIM, :] = blk.T.astype(BF16)
            dvt_ref[s, hh * VA_ROWS + DIFF_V_DIM:(hh + 1) * VA_ROWS, :] = ones_a

    skv = z[:, _C_SKV:_C_SKV + LANES]
    skn = headnorm(skv, g128_ref, gsk_ref[...])
    skk = _dup_halves(skn, first=True)
    svv = _dup_halves(skv, first=False)
    skc_ref[...] = skk[:, :HEAD_DIM]
    svc_ref[...] = svv[:, :HEAD_DIM]
    skk_ref[...] = skk.astype(BF16)
    for s in range(n_sub):
        svt_ref[s, 0:HEAD_DIM, :] = svv[s * t_sub:(s + 1) * t_sub, :].T[:HEAD_DIM, :].astype(BF16)
        svt_ref[s, HEAD_DIM:VB_ROWS, :] = ones_a

    iq_ref[...] = z[:, _C_IQ:_C_IQ + IDX_HEADS * IDX_DIM].astype(BF16)
    ikw = z[:, _C_IKW:_C_IKW + LANES]
    ikk = _dup_halves(ikw, first=True)
    ikc_ref[...] = ikk[:, :HEAD_DIM]
    ikk_ref[...] = ikk.astype(BF16)
    iwt_ref[...] = ikw.T[HEAD_DIM:HEAD_DIM + 8, :] * IDX_SCALE


def _proj(x1, g_mix, w_in, g_dq, g_dk, g_sq, g_sk):
    n, d = x1.shape
    tm = _row_tile(n)
    t_sub = min(KEY_TILE, tm)
    n_sub = tm // t_sub
    nt = n // t_sub

    def tile8(g, scale):
        return (jnp.tile(g.astype(F32), DIFF_QK // HEAD_DIM) * scale)[None, :]

    gsk_pad = jnp.concatenate([g_sk.astype(F32), jnp.ones((HEAD_DIM,), F32)])[None, :]
    grp = jnp.arange(DIFF_QK) // HEAD_DIM
    ones512 = (grp[:, None] == grp[None, :]).astype(BF16)
    ones128 = ones512[:LANES, :LANES]
    consts = [g_mix, w_in, tile8(g_dq, QK_SCALE * LOG2E), tile8(g_dk, 1.0), tile8(g_sq, QK_SCALE * LOG2E), gsk_pad,
              ones512, ones128]

    def rows(width):
        return pl.BlockSpec((tm, width), lambda i: (i, 0))

    def tposed(nrows):
        return pl.BlockSpec((n_sub, nrows, t_sub), lambda i: (i, 0, 0))

    out_shapes = [
        jax.ShapeDtypeStruct((n, DIFF_QK), F32), jax.ShapeDtypeStruct((n, DIFF_WIDTH), F32),
        jax.ShapeDtypeStruct((n, HEAD_DIM), F32), jax.ShapeDtypeStruct((n, HEAD_DIM), F32),
        jax.ShapeDtypeStruct((n, IDX_DIM), F32),
        jax.ShapeDtypeStruct((n, DIFF_QK), BF16), jax.ShapeDtypeStruct((n, DIFF_QK), BF16),
        jax.ShapeDtypeStruct((nt, DIFF_HEADS * VA_ROWS, t_sub), BF16),
        jax.ShapeDtypeStruct((n, DSA_WIDTH), BF16), jax.ShapeDtypeStruct((n, LANES), BF16),
        jax.ShapeDtypeStruct((nt, VB_ROWS, t_sub), BF16),
        jax.ShapeDtypeStruct((n, IDX_HEADS * IDX_DIM), BF16), jax.ShapeDtypeStruct((n, LANES), BF16),
        jax.ShapeDtypeStruct((8, n), F32),
    ]
    out_specs = [rows(DIFF_QK), rows(DIFF_WIDTH), rows(HEAD_DIM), rows(HEAD_DIM), rows(IDX_DIM),
                 rows(DIFF_QK), rows(DIFF_QK), tposed(DIFF_HEADS * VA_ROWS),
                 rows(DSA_WIDTH), rows(LANES), tposed(VB_ROWS),
                 rows(IDX_HEADS * IDX_DIM), rows(LANES), pl.BlockSpec((8, tm), lambda i: (0, i))]
    outs = pl.pallas_call(
        _proj_kernel, grid=(n // tm,),
        in_specs=[rows(d)] + [_const_spec(c.shape) for c in consts],
        out_specs=out_specs, out_shape=out_shapes,
        compiler_params=_params(1), name="proj")(x1, *consts)
    names = ("dk_c", "dv_c", "sk_c", "sv_c", "ik_c", "dq", "dk", "dvt", "sq", "skk", "svt", "iq", "ikk", "iwt")
    return dict(zip(names, outs))


def _half_masked_weights(q_tile, n_pairs):
    lane = lax.broadcasted_iota(I32, (q_tile.shape[0], LANES), 1)
    parts = []
    for pr in range(n_pairs):
        qp = q_tile[:, pr * LANES:(pr + 1) * LANES].astype(F32)
        parts.append(jnp.where(lane < HEAD_DIM, qp, 0.0).astype(q_tile.dtype))
        parts.append(jnp.where(lane >= HEAD_DIM, qp, 0.0).astype(q_tile.dtype))
    return jnp.concatenate(parts, axis=0)


def _softmax_step(s, m_old):
    m_new = jnp.maximum(m_old, jnp.max(s, axis=0, keepdims=True))
    alpha = jnp.exp2(m_old - m_new)
    p = jnp.exp2(s - m_new).astype(BF16)
    return m_new, alpha, p


def _run_tiles(i, step):
    def far_pair(p, carry):
        step(2 * p, 2, False)
        return carry

    if isinstance(i, int):
        n_far = max(i - 1, 0)
        lax.fori_loop(0, n_far // 2, far_pair, 0)
        if n_far % 2:
            step(n_far - 1, 1, False)
        if i >= 1:
            step(i - 1, 2, True)
        else:
            step(0, 1, True)
    else:
        n_far = jnp.maximum(i - 1, 0)
        lax.fori_loop(0, lax.shift_right_logical(n_far, 1), far_pair, 0)
        pl.when((n_far & 1) == 1)(lambda: step(n_far - 1, 1, False))
        pl.when(i >= 1)(lambda: step(i - 1, 2, True))
        pl.when(i == 0)(lambda: step(0, 1, True))


def _row_start(j, tk):
    return j * tk if isinstance(j, int) else pl.multiple_of(j * tk, tk)


def _attn_a_kernel(lamv_ref, q_ref, k_ref, vt_ref, bias_ref, gsub_ref, o_ref, m_ref, acc_ref,
                   *, q_tile_index, lam_init):
    tq = q_ref.shape[0]
    tk = vt_ref.shape[2]
    i = pl.program_id(1) if q_tile_index is None else q_tile_index
    weights = _half_masked_weights(q_ref[...], DIFF_HEADS)
    m_ref[...] = jnp.full(m_ref.shape, M_INIT, F32)
    acc_ref[...] = jnp.zeros(acc_ref.shape, F32)

    def step(j0, width, near):
        row0 = _row_start(j0, tk)
        rows = width * tk
        scores = [_dot_nt(k_ref[pl.ds(row0, rows), h * LANES:(h + 1) * LANES],
                          weights[2 * h * tq:(2 * h + 2) * tq, :])
                  for h in range(DIFF_HEADS)]
        for h in range(DIFF_HEADS):
            s = scores[h]
            if near:
                b = bias_ref[h] if width == 2 else bias_ref[h, tk:, :]
                s = s + jnp.concatenate([b, b], axis=1)
            m_new, alpha, p = _softmax_step(s, m_ref[h])
            m_ref[h] = m_new
            vt = jnp.concatenate([vt_ref[j0 + w, h * VA_ROWS:(h + 1) * VA_ROWS, :] for w in range(width)], axis=1)
            acc_ref[h] = alpha * acc_ref[h] + _dot(vt, p)

    _run_tiles(i, step)

    lv = lamv_ref[...]
    lam = (jnp.exp(jnp.sum(lv[0:1] * lv[1:2], axis=1, keepdims=True))
           - jnp.exp(jnp.sum(lv[2:3] * lv[3:4], axis=1, keepdims=True)) + lam_init)
    for h in range(DIFF_HEADS):
        acc = acc_ref[h]
        o = acc[:DIFF_V_DIM, :] / acc[DIFF_V_DIM:DIFF_V_DIM + 1, :]
        d = o[:, :tq] - lam * o[:, tq:]
        y = d * lax.rsqrt(jnp.mean(d * d, axis=0, keepdims=True) + EPS) * gsub_ref[...] * (1.0 - lam_init)
        o_ref[:, h * DIFF_V_DIM:(h + 1) * DIFF_V_DIM] = y.T.astype(BF16)


def _attn_a(lamv, q, k, vt, bias, gsub, *, batch, nq, tq, tkeys, q_tile_index, lam_init):
    tk = vt.shape[2]
    nkt = tkeys // tk
    kern = functools.partial(_attn_a_kernel, q_tile_index=q_tile_index, lam_init=lam_init)
    return pl.pallas_call(
        kern, grid=(batch, nq),
        in_specs=[_const_spec(lamv.shape),
                  pl.BlockSpec((tq, DIFF_QK), lambda b, i: (b * nq + i, 0)),
                  pl.BlockSpec((tkeys, DIFF_QK), lambda b, i: (b, 0)),
                  pl.BlockSpec((nkt, DIFF_HEADS * VA_ROWS, tk), lambda b, i: (b, 0, 0)),
                  _const_spec(bias.shape), _const_spec(gsub.shape)],
        out_specs=pl.BlockSpec((tq, DIFF_WIDTH), lambda b, i: (b * nq + i, 0)),
        out_shape=jax.ShapeDtypeStruct((batch * nq * tq, DIFF_WIDTH), BF16),
        scratch_shapes=[pltpu.VMEM((DIFF_HEADS, 1, 2 * tq), F32),
                        pltpu.VMEM((DIFF_HEADS, VA_ROWS, 2 * tq), F32)],
        compiler_params=_params(2), name="attn_a")(lamv, q, k, vt, bias, gsub)


def _key_to_f32(key):
    return pltpu.bitcast(key ^ ((key >> 31) & 0x7FFFFFFF), F32)


def _attn_b_kernel(iq_ref, ikk_ref, iwt_ref, sq_ref, skk_ref, svt_ref, bias_ref, vis_ref, o_ref,
                   sc_ref, hi_ref, m_ref, acc_ref, *, q_tile_index, n_sel, idx_bits):
    tq = iq_ref.shape[0]
    tk = svt_ref.shape[2]
    static = q_tile_index is not None
    i = q_tile_index if static else pl.program_id(1)
    n_tiles = i + 1
    n_pairs = (n_tiles + 1) // 2 if static else lax.shift_right_logical(n_tiles + 1, 1)
    neg_inf = float("-inf")

    w_idx = _half_masked_weights(iq_ref[...], IDX_HEADS // 2)
    iw = iwt_ref[...]

    def index_scores(j, visible):
        row0 = _row_start(j, tk)
        rel = _dot_nt(ikk_ref[pl.ds(row0, tk), :], w_idx)
        score = None
        for h in range(IDX_HEADS):
            term = iw[h:h + 1, :] * jnp.maximum(rel[:, h * tq:(h + 1) * tq], 0.0)
            score = term if score is None else score + term
        if visible is not None:
            score = jnp.where(visible > 0.0, score, neg_inf)
        sc_ref[pl.ds(row0, tk), :] = score
        top = pltpu.bitcast(pltpu.bitcast(score, I32) & jnp.int32(-65536), F32)
        hi_ref[pl.ds(row0, tk), :] = top.astype(jnp.bfloat16)

    def idx_body(j, carry):
        index_scores(j, None)
        return carry

    lax.fori_loop(0, i, idx_body, 0)
    index_scores(i, vis_ref[...])
    pad0 = _row_start(n_tiles, tk)
    sc_ref[pl.ds(pad0, tk), :] = jnp.full((tk, tq), neg_inf, F32)
    hi_ref[pl.ds(pad0, tk), :] = jnp.full((tk, tq), neg_inf, jnp.bfloat16)

    pair_iota = lax.broadcasted_iota(I32, (2 * tk, tq), 0)
    n_slab = 2 * tk // COUNT_ROWS

    def count(pred):
        def body(pp, acc):
            row0 = pl.multiple_of(pp * (2 * tk), 2 * tk)
            hit = pred(sc_ref[pl.ds(row0, 2 * tk), :], row0).astype(F32)
            return acc + jnp.sum(hit.reshape(n_slab, COUNT_ROWS, tq), axis=0)
        acc = lax.fori_loop(0, n_pairs, body, jnp.zeros((COUNT_ROWS, tq), F32))
        return jnp.sum(acc, axis=0, keepdims=True)

    def count_top(cand_bf):
        one, zero = jnp.ones((), jnp.bfloat16), jnp.zeros((), jnp.bfloat16)

        def body(pp, acc):
            row0 = pl.multiple_of(pp * (2 * tk), 2 * tk)
            hit = jnp.where(hi_ref[pl.ds(row0, 2 * tk), :] >= cand_bf, one, zero)
            slabs = [hit[r * COUNT_ROWS:(r + 1) * COUNT_ROWS] for r in range(n_slab)]
            while len(slabs) > 1:
                slabs = [a + b for a, b in zip(slabs[0::2], slabs[1::2])]
            return acc + slabs[0]
        acc = lax.fori_loop(0, n_pairs, body, jnp.zeros((COUNT_ROWS, tq), jnp.bfloat16))
        return jnp.sum(acc.astype(F32), axis=0, keepdims=True)

    def accept(cnt, cand, carry):
        t, cnt_t = carry
        ok = (cnt >= n_sel) | (cand < KEY_NEG_INF)
        return jnp.where(ok, cand, t), jnp.where(ok, cnt, cnt_t)

    def top_bit(p, carry):
        cand = carry[0] + jnp.left_shift(jnp.int32(1), 31 - p)
        cand_bf = pltpu.bitcast(cand ^ ((cand >> 31) & 0x7FFF0000), F32).astype(jnp.bfloat16)
        return accept(count_top(cand_bf), cand, carry)

    def low_bit(p, carry):
        cand = carry[0] + jnp.left_shift(jnp.int32(1), 16 - p)
        cand_f = _key_to_f32(cand)
        return accept(count(lambda tile, _: tile >= cand_f), cand, carry)

    start = (jnp.full((1, tq), INT_MIN, I32), jnp.zeros((1, tq), F32))
    top_key, top_cnt = lax.fori_loop(0, 16, top_bit, start)
    top_key = jnp.where(top_key == 0, jnp.int32(-65536), top_key)
    thr_key, cnt_thr = lax.fori_loop(0, 17, low_bit, (top_key, top_cnt))
    thr = _key_to_f32(thr_key)
    cut = (thr > neg_inf) & (cnt_thr > n_sel)
    thr = jnp.maximum(thr, jnp.finfo(F32).min)

    def tie_limit():
        above = count(lambda tile, _: tile > thr)
        need = jnp.where(cut, n_sel - above, jnp.float32(2 ** 30))

        def idx_bit(p, lim):
            cand = lim + jnp.left_shift(jnp.int32(1), idx_bits - 1 - p)
            cnt = count(lambda tile, row0: (tile == thr) & (pair_iota + row0 < cand))
            return jnp.where(cnt < need, cand, lim)

        return lax.fori_loop(0, idx_bits, idx_bit, jnp.zeros((1, tq), I32))

    any_cut = jnp.max(cut.astype(I32)) > 0
    last = lax.cond(any_cut, tie_limit, lambda: jnp.full((1, tq), 2 ** 30, I32))

    w_q = _half_masked_weights(sq_ref[...], DSA_HEADS // 2)
    m_ref[...] = jnp.full(m_ref.shape, M_INIT, F32)
    acc_ref[...] = jnp.zeros(acc_ref.shape, F32)

    def step(j0, width, near):
        row0 = _row_start(j0, tk)
        rows = width * tk
        s_all = _dot_nt(skk_ref[pl.ds(row0, rows), :], w_q)
        sc = sc_ref[pl.ds(row0, rows), :]
        key_index = lax.broadcasted_iota(I32, (rows, tq), 0) + row0
        sel = (sc > thr) | ((sc == thr) & (key_index <= last))
        parts = []
        for h in range(DSA_HEADS):
            s = s_all[:, h * tq:(h + 1) * tq]
            if near:
                s = s + (bias_ref[h] if width == 2 else bias_ref[h, tk:, :])
            parts.append(jnp.where(sel, s, S_MASKED))
        m_new, alpha, p = _softmax_step(jnp.concatenate(parts, axis=1), m_ref[...])
        vt = jnp.concatenate([svt_ref[j0 + w] for w in range(width)], axis=1)
        acc_ref[...] = alpha * acc_ref[...] + _dot(vt, p)
        m_ref[...] = m_new

    _run_tiles(i, step)

    acc = acc_ref[...]
    o = acc[:HEAD_DIM, :] / acc[HEAD_DIM:HEAD_DIM + 1, :]
    for pr in range(DSA_HEADS // 2):
        pair = jnp.concatenate([o[:, 2 * pr * tq:(2 * pr + 1) * tq], o[:, (2 * pr + 1) * tq:(2 * pr + 2) * tq]],
                               axis=0)
        o_ref[:, pr * LANES:(pr + 1) * LANES] = pair.T.astype(BF16)


def _attn_b(p, bias, vis, *, batch, nq, tq, tkeys, q_tile_index, n_sel):
    tk = p["svt"].shape[2]
    nkt = tkeys // tk
    idx_bits = max(1, (tkeys + tk - 1).bit_length())
    assert (tkeys + tk) // COUNT_ROWS <= 256, "bf16 partial counts must stay exact"
    kern = functools.partial(_attn_b_kernel, q_tile_index=q_tile_index, n_sel=n_sel, idx_bits=idx_bits)
    qrow = lambda b, i: (b * nq + i, 0)
    krow = lambda b, i: (b, 0)
    return pl.pallas_call(
        kern, grid=(batch, nq),
        in_specs=[pl.BlockSpec((tq, IDX_HEADS * IDX_DIM), qrow),
                  pl.BlockSpec((tkeys, LANES), krow),
                  pl.BlockSpec((8, tq), lambda b, i: (0, b * nq + i)),
                  pl.BlockSpec((tq, DSA_WIDTH), qrow),
                  pl.BlockSpec((tkeys, LANES), krow),
                  pl.BlockSpec((nkt, VB_ROWS, tk), lambda b, i: (b, 0, 0)),
                  _const_spec(bias.shape), _const_spec(vis.shape)],
        out_specs=pl.BlockSpec((tq, DSA_WIDTH), qrow),
        out_shape=jax.ShapeDtypeStruct((batch * nq * tq, DSA_WIDTH), BF16),
        scratch_shapes=[pltpu.VMEM((tkeys + tk, tq), F32),
                        pltpu.VMEM((tkeys + tk, tq), jnp.bfloat16),
                        pltpu.VMEM((1, DSA_HEADS * tq), F32),
                        pltpu.VMEM((VB_ROWS, DSA_HEADS * tq), F32)],
        compiler_params=_params(2), name="attn_b")(
            p["iq"], p["ikk"], p["iwt"], p["sq"], p["skk"], p["svt"], bias, vis)


def _t5_bucket(rel):
    half = NUM_BUCKETS // 2
    exact = half // 2
    n = jnp.abs(rel)
    logn = jnp.log(jnp.maximum(n, 1).astype(F32) / exact)
    large = exact + (logn / math.log(MAX_DISTANCE / exact) * (half - exact)).astype(I32)
    large = jnp.minimum(large, half - 1)
    return jnp.where(rel > 0, half, 0) + jnp.where(n < exact, n, large)


def _bias_lookup(rel_bias, bucket):
    out = jnp.zeros((rel_bias.shape[1],) + bucket.shape, F32)
    for b in range(NUM_BUCKETS):
        out = out + jnp.where(bucket[None] == b, rel_bias[b].astype(F32)[:, None, None], 0.0)
    return out


def _bias_tiles(rel_bias, tk, tq, q_valid, k_valid_diag):
    kk = jnp.arange(tk, dtype=I32)[:, None]
    qq = jnp.arange(tq, dtype=I32)[None, :]
    far = _bias_lookup(rel_bias, _t5_bucket(jnp.full((1, 1), -(2 * MAX_DISTANCE), I32)))
    tiles = [(_bias_lookup(rel_bias, _t5_bucket(kk + d - qq)) - far) * LOG2E for d in (-tk, 0)]
    vis = ((kk // CHUNK) <= (qq // CHUNK)) & (kk < k_valid_diag) | (qq >= q_valid)
    tiles[1] = jnp.where(vis[None], tiles[1], M_INIT)
    return jnp.concatenate(tiles, axis=1), vis.astype(F32)


def _token_mix(p, rel_bias, lamv, gsub, *, batch, nq, tq, tkeys, q_tile_index, q_valid, k_valid_diag,
               n_sel, lam_init):
    bias, vis = _bias_tiles(rel_bias, KEY_TILE, tq, q_valid, k_valid_diag)
    gsub_b = jnp.broadcast_to(gsub.astype(F32)[:, None], (DIFF_V_DIM, tq))
    o_a = _attn_a(lamv, p["dq"], p["dk"], p["dvt"], bias[:DIFF_HEADS], gsub_b, batch=batch, nq=nq, tq=tq,
                  tkeys=tkeys, q_tile_index=q_tile_index, lam_init=lam_init)
    o_b = _attn_b(p, bias[DIFF_HEADS:], vis, batch=batch, nq=nq, tq=tq, tkeys=tkeys,
                  q_tile_index=q_tile_index, n_sel=n_sel)
    return o_a, o_b


def _pad_rows(a, rows):
    return jnp.pad(a, ((0, 0), (0, rows - a.shape[1])) + ((0, 0),) * (a.ndim - 2))


def kernel(x_prompt, x_sample, cache_diff_k, cache_diff_v, cache_dsa_k, cache_dsa_v, cache_idx_k, g_ffn1, w1_gate, w1_up, w1_down, g_mix, w_in, g_dq, g_dk, g_sq, g_sk, lam_q1, lam_k1, lam_q2, lam_k2, g_subln, w_out, g_ffn2, w2_gate, w2_up, w2_down, rel_bias):
    depth = g_ffn1.shape[0]
    assert depth == 1, "one layer"
    l = 0
    batch, seq, d_model = x_prompt.shape
    sbatch, sseq, _ = x_sample.shape
    past = cache_diff_k.shape[2]
    tk = KEY_TILE
    assert seq % Q_TILE == 0 and Q_TILE % CHUNK == 0 and Q_TILE == tk
    assert past % tk == 0 and past % CHUNK == 0 and sseq <= CHUNK and sseq % 8 == 0
    lam_init = 0.8 - 0.6 * math.exp(-0.3 * l)

    bf = lambda w: w.astype(BF16)
    row = lambda g: g.astype(F32)[None, :]
    w1 = (row(g_ffn1[l]), bf(w1_gate[l]), bf(w1_up[l]), bf(w1_down[l]))
    w2 = (row(g_ffn2[l]), bf(w2_gate[l]), bf(w2_up[l]), bf(w2_down[l]))
    w_in_p = jnp.pad(bf(w_in[l]), ((0, 0), (0, IN_COLS_PAD - IN_COLS)))
    wo_a, wo_b = bf(w_out[l][:DIFF_WIDTH]), bf(w_out[l][DIFF_WIDTH:])
    lamv = jnp.pad(jnp.stack([lam_q1[l], lam_k1[l], lam_q2[l], lam_k2[l]]).astype(F32), ((0, 4), (0, LANES - HEAD_DIM)))
    proj = functools.partial(_proj, g_mix=row(g_mix[l]), w_in=w_in_p, g_dq=g_dq[l], g_dk=g_dk[l], g_sq=g_sq[l],
                             g_sk=g_sk[l])

    xp1 = _ffn(x_prompt.reshape(batch * seq, d_model), *w1)
    pp = proj(xp1)
    oa_p, ob_p = _token_mix(pp, rel_bias, lamv, g_subln[l], batch=batch, nq=seq // Q_TILE, tq=Q_TILE, tkeys=seq,
                            q_tile_index=None, q_valid=Q_TILE, k_valid_diag=tk,
                            n_sel=min(TOPK_MAX, seq // 4), lam_init=lam_init)
    yp = _out_ffn(xp1, oa_p, ob_p, wo_a, wo_b, *w2).reshape(batch, seq, d_model)

    ns = sbatch * sseq
    xs1 = _ffn(x_sample.reshape(ns, d_model), *w1)
    ps = proj(xs1)
    tq_s = LANES
    keys_s = past + tk
    per_b = lambda a: a.reshape(sbatch, sseq, a.shape[-1])

    def q_rows(a):
        return _pad_rows(per_b(a), tq_s).reshape(sbatch * tq_s, a.shape[-1])

    def k_rows(cache, new):
        both = jnp.concatenate([cache.astype(BF16), per_b(new)], axis=1)
        return _pad_rows(both, keys_s).reshape(sbatch * keys_s, new.shape[-1])

    def v_tiles(cache, new_t, rows, heads):
        rv = rows - ONES_ROWS
        c = cache.astype(BF16).reshape(sbatch, past // tk, tk, heads, rv)
        c = jnp.transpose(c, (0, 1, 3, 4, 2))
        c = jnp.concatenate([c, jnp.ones((sbatch, past // tk, heads, ONES_ROWS, tk), BF16)], axis=3)
        c = c.reshape(sbatch, past // tk, heads * rows, tk)
        n = new_t.reshape(heads * rows, sbatch, sseq)
        n = jnp.pad(jnp.transpose(n, (1, 0, 2)), ((0, 0), (0, 0), (0, tk - sseq)))[:, None]
        return jnp.concatenate([c, n], axis=1).reshape(sbatch * (past // tk + 1), heads * rows, tk)

    dup = lambda c: jnp.concatenate([c, c], axis=-1)
    iwt_s = jnp.pad(ps["iwt"].reshape(8, sbatch, sseq), ((0, 0), (0, 0), (0, tq_s - sseq))).reshape(8, sbatch * tq_s)
    psa = {
        "dq": q_rows(ps["dq"]), "sq": q_rows(ps["sq"]), "iq": q_rows(ps["iq"]), "iwt": iwt_s,
        "dk": k_rows(cache_diff_k[l].reshape(sbatch, past, DIFF_QK), ps["dk"]),
        "skk": k_rows(dup(cache_dsa_k[l]), ps["skk"]),
        "ikk": k_rows(dup(cache_idx_k[l]), ps["ikk"]),
        "dvt": v_tiles(cache_diff_v[l].reshape(sbatch, past, DIFF_WIDTH), ps["dvt"], VA_ROWS, DIFF_HEADS),
        "svt": v_tiles(cache_dsa_v[l], ps["svt"], VB_ROWS, 1),
    }
    oa_s, ob_s = _token_mix(psa, rel_bias, lamv, g_subln[l], batch=sbatch, nq=1, tq=tq_s, tkeys=keys_s,
                            q_tile_index=past // tk, q_valid=sseq, k_valid_diag=sseq,
                            n_sel=min(TOPK_MAX, (past + sseq) // 4), lam_init=lam_init)
    take = lambda o: o.reshape(sbatch, tq_s, o.shape[-1])[:, :sseq].reshape(ns, o.shape[-1])
    ys = _out_ffn(xs1, take(oa_s), take(ob_s), wo_a, wo_b, *w2).reshape(sbatch, sseq, d_model)

    def caches(p, b, t):
        return (p["dk_c"].reshape(depth, b, t, DIFF_HEADS, 2, HEAD_DIM),
                p["dv_c"].reshape(depth, b, t, DIFF_HEADS, DIFF_V_DIM),
                p["sk_c"].reshape(depth, b, t, HEAD_DIM), p["sv_c"].reshape(depth, b, t, HEAD_DIM),
                p["ik_c"].reshape(depth, b, t, IDX_DIM))

    return (yp, ys) + caches(pp, batch, seq) + caches(ps, sbatch, sseq)
```

```python
import functools
import math

import jax
import jax.numpy as jnp
from jax import lax
from jax.experimental import pallas as pl
from jax.experimental.pallas import tpu as pltpu

F32 = jnp.float32
BF16 = jnp.bfloat16
I32 = jnp.int32

CHUNK = 64
HEAD_DIM = 64
DIFF_HEADS = 4
DIFF_V_DIM = 2 * HEAD_DIM
DSA_HEADS = 8
IDX_HEADS = 4
IDX_DIM = 64
TOPK_MAX = 256
NUM_BUCKETS = 32
MAX_DISTANCE = 128
EPS = 1e-6
IDX_SCALE = (IDX_HEADS ** -0.5) * (IDX_DIM ** -0.5)
QK_SCALE = HEAD_DIM ** -0.5
LOG2E = math.log2(math.e)

DIFF_QK = DIFF_HEADS * 2 * HEAD_DIM
DIFF_WIDTH = DIFF_HEADS * DIFF_V_DIM
DSA_WIDTH = DSA_HEADS * HEAD_DIM
IN_COLS = 3 * DIFF_QK + DSA_WIDTH + 2 * HEAD_DIM + IDX_HEADS * IDX_DIM + IDX_DIM + IDX_HEADS

LANES = 128
BF16_SUBLANES = 16
MXU_COLS = 256
VMEM_LIMIT_BYTES = 60000 * 1024

IN_COLS_PAD = -(-IN_COLS // MXU_COLS) * MXU_COLS
ONES_ROWS = BF16_SUBLANES
VA_ROWS = DIFF_V_DIM + ONES_ROWS
VB_ROWS = HEAD_DIM + ONES_ROWS

KEY_TILE = 256
Q_TILE = 256
M_INIT = -1e30
S_MASKED = -2e30
INT_MIN = -(2 ** 31)
KEY_NEG_INF = INT_MIN + 0x7FFFFF
COUNT_ROWS = 32

_C_DQ, _C_DK, _C_DV, _C_SQ = 0, DIFF_QK, 2 * DIFF_QK, 3 * DIFF_QK
_C_SKV = _C_SQ + DSA_WIDTH
_C_IQ = _C_SKV + 2 * HEAD_DIM
_C_IKW = _C_IQ + IDX_HEADS * IDX_DIM


def _params(n_axes):
    return pltpu.CompilerParams(dimension_semantics=("arbitrary",) * n_axes,
                                vmem_limit_bytes=VMEM_LIMIT_BYTES)


def _const_spec(shape):
    nd = len(shape)
    return pl.BlockSpec(shape, lambda *_: (0,) * nd, pipeline_mode=pl.Buffered(1))


def _rms(x, g):
    return x * lax.rsqrt(jnp.mean(x * x, axis=-1, keepdims=True) + EPS) * g


def _dot(a, b):
    return jnp.dot(a, b, preferred_element_type=F32)


def _dot_nt(a, b):
    return lax.dot_general(a, b, (((1,), (1,)), ((), ())), preferred_element_type=F32)


def _ff_chunks(d_ff):
    step = 4 * MXU_COLS
    return [(s, min(step, d_ff - s)) for s in range(0, d_ff, step)]


def _swiglu_half(x, g_ref, wg_ref, wu_ref, wd_ref):
    h = _rms(x, g_ref[...]).astype(BF16)
    acc = None
    for start, width in _ff_chunks(wg_ref.shape[1]):
        gate = _dot(h, wg_ref[:, start:start + width])
        up = _dot(h, wu_ref[:, start:start + width])
        act = (gate * (1.0 / (1.0 + jnp.exp(-gate))) * up).astype(BF16)
        part = _dot(act, wd_ref[start:start + width, :])
        acc = part if acc is None else acc + part
    return x + 0.5 * acc


def _ffn_kernel(x_ref, g_ref, wg_ref, wu_ref, wd_ref, o_ref):
    o_ref[...] = _swiglu_half(x_ref[...], g_ref, wg_ref, wu_ref, wd_ref)


def _out_ffn_kernel(x_ref, oa_ref, ob_ref, woa_ref, wob_ref, g_ref, wg_ref, wu_ref, wd_ref, o_ref):
    x = x_ref[...] + _dot(oa_ref[...], woa_ref[...]) + _dot(ob_ref[...], wob_ref[...])
    o_ref[...] = _swiglu_half(x, g_ref, wg_ref, wu_ref, wd_ref)


def _row_tile(n):
    for tm in (512, 256, 128, 64, 32, 16, 8):
        if n % tm == 0:
            return tm
    raise ValueError(f"row count {n} is not a multiple of 8")


def _ffn(x, g, wg, wu, wd):
    n, d = x.shape
    tm = _row_tile(n)
    row = pl.BlockSpec((tm, d), lambda i: (i, 0))
    return pl.pallas_call(
        _ffn_kernel, grid=(n // tm,),
        in_specs=[row, _const_spec(g.shape), _const_spec(wg.shape), _const_spec(wu.shape), _const_spec(wd.shape)],
        out_specs=row, out_shape=jax.ShapeDtypeStruct((n, d), F32),
        compiler_params=_params(1), name="ffn")(x, g, wg, wu, wd)


def _out_ffn(x, oa, ob, woa, wob, g, wg, wu, wd):
    n, d = x.shape
    tm = _row_tile(n)
    row = pl.BlockSpec((tm, d), lambda i: (i, 0))
    half = pl.BlockSpec((tm, oa.shape[1]), lambda i: (i, 0))
    consts = [woa, wob, g, wg, wu, wd]
    return pl.pallas_call(
        _out_ffn_kernel, grid=(n // tm,),
        in_specs=[row, half, half] + [_const_spec(c.shape) for c in consts],
        out_specs=row, out_shape=jax.ShapeDtypeStruct((n, d), F32),
        compiler_params=_params(1), name="out_ffn")(x, oa, ob, *consts)


def _dup_halves(t, first):
    lane = lax.broadcasted_iota(I32, t.shape, 1)
    rolled = pltpu.roll(t, HEAD_DIM, axis=1)
    if first:
        return jnp.where(lane < HEAD_DIM, t, rolled)
    return jnp.where(lane < HEAD_DIM, rolled, t)


def _proj_kernel(x_ref, g_ref, win_ref, gdq_ref, gdk_ref, gsq_ref, gsk_ref, g512_ref, g128_ref,
                 dkc_ref, dvc_ref, skc_ref, svc_ref, ikc_ref,
                 dq_ref, dkb_ref, dvt_ref, sq_ref, skk_ref, svt_ref, iq_ref, ikk_ref, iwt_ref):
    h = _rms(x_ref[...], g_ref[...]).astype(BF16)
    z = _dot(h, win_ref[...])
    n_sub, _, t_sub = dvt_ref.shape

    def headnorm(seg, ones_ref, gain):
        ss = _dot((seg * seg).astype(BF16), ones_ref[...])
        return seg * lax.rsqrt(ss * (1.0 / HEAD_DIM) + EPS) * gain

    dq_ref[...] = headnorm(z[:, _C_DQ:_C_DQ + DIFF_QK], g512_ref, gdq_ref[...]).astype(BF16)
    dk = headnorm(z[:, _C_DK:_C_DK + DIFF_QK], g512_ref, gdk_ref[...])
    dkc_ref[...] = dk
    dkb_ref[...] = dk.astype(BF16)
    sq_ref[...] = headnorm(z[:, _C_SQ:_C_SQ + DSA_WIDTH], g512_ref, gsq_ref[...]).astype(BF16)

    dv = z[:, _C_DV:_C_DV + DIFF_WIDTH]
    dvc_ref[...] = dv
    ones_a = jnp.ones((ONES_ROWS, t_sub), BF16)
    for s in range(n_sub):
        for hh in range(DIFF_HEADS):
            blk = dv[s * t_sub:(s + 1) * t_sub, hh * DIFF_V_DIM:(hh + 1) * DIFF_V_DIM]
            dvt_ref[s, hh * VA_ROWS:hh * VA_ROWS + DIFF_V_DIM, :] = blk.T.astype(BF16)
            dvt_ref[s, hh * VA_ROWS + DIFF_V_DIM:(hh + 1) * VA_ROWS, :] = ones_a

    skv = z[:, _C_SKV:_C_SKV + LANES]
    skn = headnorm(skv, g128_ref, gsk_ref[...])
    skk = _dup_halves(skn, first=True)
    svv = _dup_halves(skv, first=False)
    skc_ref[...] = skk[:, :HEAD_DIM]
    svc_ref[...] = svv[:, :HEAD_DIM]
    skk_ref[...] = skk.astype(BF16)
    for s in range(n_sub):
        svt_ref[s, 0:HEAD_DIM, :] = svv[s * t_sub:(s + 1) * t_sub, :].T[:HEAD_DIM, :].astype(BF16)
        svt_ref[s, HEAD_DIM:VB_ROWS, :] = ones_a

    iq_ref[...] = z[:, _C_IQ:_C_IQ + IDX_HEADS * IDX_DIM].astype(BF16)
    ikw = z[:, _C_IKW:_C_IKW + LANES]
    ikk = _dup_halves(ikw, first=True)
    ikc_ref[...] = ikk[:, :HEAD_DIM]
    ikk_ref[...] = ikk.astype(BF16)
    iwt_ref[...] = ikw.T[HEAD_DIM:HEAD_DIM + 8, :] * IDX_SCALE


def _proj(x1, g_mix, w_in, g_dq, g_dk, g_sq, g_sk):
    n, d = x1.shape
    tm = _row_tile(n)
    t_sub = min(KEY_TILE, tm)
    n_sub = tm // t_sub
    nt = n // t_sub

    def tile8(g, scale):
        return (jnp.tile(g.astype(F32), DIFF_QK // HEAD_DIM) * scale)[None, :]

    gsk_pad = jnp.concatenate([g_sk.astype(F32), jnp.ones((HEAD_DIM,), F32)])[None, :]
    grp = jnp.arange(DIFF_QK) // HEAD_DIM
    ones512 = (grp[:, None] == grp[None, :]).astype(BF16)
    ones128 = ones512[:LANES, :LANES]
    consts = [g_mix, w_in, tile8(g_dq, QK_SCALE * LOG2E), tile8(g_dk, 1.0), tile8(g_sq, QK_SCALE * LOG2E), gsk_pad,
              ones512, ones128]

    def rows(width):
        return pl.BlockSpec((tm, width), lambda i: (i, 0))

    def tposed(nrows):
        return pl.BlockSpec((n_sub, nrows, t_sub), lambda i: (i, 0, 0))

    out_shapes = [
        jax.ShapeDtypeStruct((n, DIFF_QK), F32), jax.ShapeDtypeStruct((n, DIFF_WIDTH), F32),
        jax.ShapeDtypeStruct((n, HEAD_DIM), F32), jax.ShapeDtypeStruct((n, HEAD_DIM), F32),
        jax.ShapeDtypeStruct((n, IDX_DIM), F32),
        jax.ShapeDtypeStruct((n, DIFF_QK), BF16), jax.ShapeDtypeStruct((n, DIFF_QK), BF16),
        jax.ShapeDtypeStruct((nt, DIFF_HEADS * VA_ROWS, t_sub), BF16),
        jax.ShapeDtypeStruct((n, DSA_WIDTH), BF16), jax.ShapeDtypeStruct((n, LANES), BF16),
        jax.ShapeDtypeStruct((nt, VB_ROWS, t_sub), BF16),
        jax.ShapeDtypeStruct((n, IDX_HEADS * IDX_DIM), BF16), jax.ShapeDtypeStruct((n, LANES), BF16),
        jax.ShapeDtypeStruct((8, n), F32),
    ]
    out_specs = [rows(DIFF_QK), rows(DIFF_WIDTH), rows(HEAD_DIM), rows(HEAD_DIM), rows(IDX_DIM),
                 rows(DIFF_QK), rows(DIFF_QK), tposed(DIFF_HEADS * VA_ROWS),
                 rows(DSA_WIDTH), rows(LANES), tposed(VB_ROWS),
                 rows(IDX_HEADS * IDX_DIM), rows(LANES), pl.BlockSpec((8, tm), lambda i: (0, i))]
    outs = pl.pallas_call(
        _proj_kernel, grid=(n // tm,),
        in_specs=[rows(d)] + [_const_spec(c.shape) for c in consts],
        out_specs=out_specs, out_shape=out_shapes,
        compiler_params=_params(1), name="proj")(x1, *consts)
    names = ("dk_c", "dv_c", "sk_c", "sv_c", "ik_c", "dq", "dk", "dvt", "sq", "skk", "svt", "iq", "ikk", "iwt")
    return dict(zip(names, outs))


def _half_masked_weights(q_tile, n_pairs):
    lane = lax.broadcasted_iota(I32, (q_tile.shape[0], LANES), 1)
    parts = []
    for pr in range(n_pairs):
        qp = q_tile[:, pr * LANES:(pr + 1) * LANES].astype(F32)
        parts.append(jnp.where(lane < HEAD_DIM, qp, 0.0).astype(q_tile.dtype))
        parts.append(jnp.where(lane >= HEAD_DIM, qp, 0.0).astype(q_tile.dtype))
    return jnp.concatenate(parts, axis=0)


def _softmax_step(s, m_old):
    m_new = jnp.maximum(m_old, jnp.max(s, axis=0, keepdims=True))
    alpha = jnp.exp2(m_old - m_new)
    p = jnp.exp2(s - m_new).astype(BF16)
    return m_new, alpha, p


def _run_tiles(i, step):
    def far_pair(p, carry):
        step(2 * p, 2, False)
        return carry

    if isinstance(i, int):
        n_far = max(i - 1, 0)
        lax.fori_loop(0, n_far // 2, far_pair, 0)
        if n_far % 2:
            step(n_far - 1, 1, False)
        if i >= 1:
            step(i - 1, 2, True)
        else:
            step(0, 1, True)
    else:
        n_far = jnp.maximum(i - 1, 0)
        lax.fori_loop(0, lax.shift_right_logical(n_far, 1), far_pair, 0)
        pl.when((n_far & 1) == 1)(lambda: step(n_far - 1, 1, False))
        pl.when(i >= 1)(lambda: step(i - 1, 2, True))
        pl.when(i == 0)(lambda: step(0, 1, True))


def _row_start(j, tk):
    return j * tk if isinstance(j, int) else pl.multiple_of(j * tk, tk)


def _attn_a_kernel(lamv_ref, q_ref, k_ref, vt_ref, bias_ref, gsub_ref, o_ref, m_ref, acc_ref,
                   *, q_tile_index, lam_init):
    tq = q_ref.shape[0]
    tk = vt_ref.shape[2]
    i = pl.program_id(1) if q_tile_index is None else q_tile_index
    weights = _half_masked_weights(q_ref[...], DIFF_HEADS)
    m_ref[...] = jnp.full(m_ref.shape, M_INIT, F32)
    acc_ref[...] = jnp.zeros(acc_ref.shape, F32)

    def step(j0, width, near):
        row0 = _row_start(j0, tk)
        rows = width * tk
        scores = [_dot_nt(k_ref[pl.ds(row0, rows), h * LANES:(h + 1) * LANES],
                          weights[2 * h * tq:(2 * h + 2) * tq, :])
                  for h in range(DIFF_HEADS)]
        for h in range(DIFF_HEADS):
            s = scores[h]
            if near:
                b = bias_ref[h] if width == 2 else bias_ref[h, tk:, :]
                s = s + jnp.concatenate([b, b], axis=1)
            m_new, alpha, p = _softmax_step(s, m_ref[h])
            m_ref[h] = m_new
            vt = jnp.concatenate([vt_ref[j0 + w, h * VA_ROWS:(h + 1) * VA_ROWS, :] for w in range(width)], axis=1)
            acc_ref[h] = alpha * acc_ref[h] + _dot(vt, p)

    _run_tiles(i, step)

    lv = lamv_ref[...]
    lam = (jnp.exp(jnp.sum(lv[0:1] * lv[1:2], axis=1, keepdims=True))
           - jnp.exp(jnp.sum(lv[2:3] * lv[3:4], axis=1, keepdims=True)) + lam_init)
    for h in range(DIFF_HEADS):
        acc = acc_ref[h]
        o = acc[:DIFF_V_DIM, :] / acc[DIFF_V_DIM:DIFF_V_DIM + 1, :]
        d = o[:, :tq] - lam * o[:, tq:]
        y = d * lax.rsqrt(jnp.mean(d * d, axis=0, keepdims=True) + EPS) * gsub_ref[...] * (1.0 - lam_init)
        o_ref[:, h * DIFF_V_DIM:(h + 1) * DIFF_V_DIM] = y.T.astype(BF16)


def _attn_a(lamv, q, k, vt, bias, gsub, *, batch, nq, tq, tkeys, q_tile_index, lam_init):
    tk = vt.shape[2]
    nkt = tkeys // tk
    kern = functools.partial(_attn_a_kernel, q_tile_index=q_tile_index, lam_init=lam_init)
    return pl.pallas_call(
        kern, grid=(batch, nq),
        in_specs=[_const_spec(lamv.shape),
                  pl.BlockSpec((tq, DIFF_QK), lambda b, i: (b * nq + i, 0)),
                  pl.BlockSpec((tkeys, DIFF_QK), lambda b, i: (b, 0)),
                  pl.BlockSpec((nkt, DIFF_HEADS * VA_ROWS, tk), lambda b, i: (b, 0, 0)),
                  _const_spec(bias.shape), _const_spec(gsub.shape)],
        out_specs=pl.BlockSpec((tq, DIFF_WIDTH), lambda b, i: (b * nq + i, 0)),
        out_shape=jax.ShapeDtypeStruct((batch * nq * tq, DIFF_WIDTH), BF16),
        scratch_shapes=[pltpu.VMEM((DIFF_HEADS, 1, 2 * tq), F32),
                        pltpu.VMEM((DIFF_HEADS, VA_ROWS, 2 * tq), F32)],
        compiler_params=_params(2), name="attn_a")(lamv, q, k, vt, bias, gsub)


def _key_to_f32(key):
    return pltpu.bitcast(key ^ ((key >> 31) & 0x7FFFFFFF), F32)


def _attn_b_kernel(iq_ref, ikk_ref, iwt_ref, sq_ref, skk_ref, svt_ref, bias_ref, vis_ref, tri_ref, o_ref,
                   sc_ref, hi_ref, m_ref, acc_ref, tie_ref, *, q_tile_index, n_sel):
    tq = iq_ref.shape[0]
    tk = svt_ref.shape[2]
    static = q_tile_index is not None
    i = q_tile_index if static else pl.program_id(1)
    n_tiles = i + 1
    n_pairs = (n_tiles + 1) // 2 if static else lax.shift_right_logical(n_tiles + 1, 1)
    neg_inf = float("-inf")

    w_idx = _half_masked_weights(iq_ref[...], IDX_HEADS // 2)
    iw = iwt_ref[...]

    def index_scores(j, visible):
        row0 = _row_start(j, tk)
        rel = _dot_nt(ikk_ref[pl.ds(row0, tk), :], w_idx)
        score = None
        for h in range(IDX_HEADS):
            term = iw[h:h + 1, :] * jnp.maximum(rel[:, h * tq:(h + 1) * tq], 0.0)
            score = term if score is None else score + term
        if visible is not None:
            score = jnp.where(visible > 0.0, score, neg_inf)
        sc_ref[pl.ds(row0, tk), :] = score
        top = pltpu.bitcast(pltpu.bitcast(score, I32) & jnp.int32(-65536), F32)
        hi_ref[pl.ds(row0, tk), :] = top.astype(jnp.bfloat16)

    def idx_body(j, carry):
        index_scores(j, None)
        return carry

    lax.fori_loop(0, i, idx_body, 0)
    index_scores(i, vis_ref[...])
    pad0 = _row_start(n_tiles, tk)
    sc_ref[pl.ds(pad0, tk), :] = jnp.full((tk, tq), neg_inf, F32)
    hi_ref[pl.ds(pad0, tk), :] = jnp.full((tk, tq), neg_inf, jnp.bfloat16)

    n_slab = 2 * tk // COUNT_ROWS

    def count(pred):
        def body(pp, acc):
            row0 = pl.multiple_of(pp * (2 * tk), 2 * tk)
            hit = pred(sc_ref[pl.ds(row0, 2 * tk), :], row0).astype(F32)
            return acc + jnp.sum(hit.reshape(n_slab, COUNT_ROWS, tq), axis=0)
        acc = lax.fori_loop(0, n_pairs, body, jnp.zeros((COUNT_ROWS, tq), F32))
        return jnp.sum(acc, axis=0, keepdims=True)

    def count_top(cand_bf):
        one, zero = jnp.ones((), jnp.bfloat16), jnp.zeros((), jnp.bfloat16)

        def body(pp, acc):
            row0 = pl.multiple_of(pp * (2 * tk), 2 * tk)
            hit = jnp.where(hi_ref[pl.ds(row0, 2 * tk), :] >= cand_bf, one, zero)
            slabs = [hit[r * COUNT_ROWS:(r + 1) * COUNT_ROWS] for r in range(n_slab)]
            while len(slabs) > 1:
                slabs = [a + b for a, b in zip(slabs[0::2], slabs[1::2])]
            return acc + slabs[0]
        acc = lax.fori_loop(0, n_pairs, body, jnp.zeros((COUNT_ROWS, tq), jnp.bfloat16))
        return jnp.sum(acc.astype(F32), axis=0, keepdims=True)

    def accept(cnt, cand, t):
        ok = (cnt >= n_sel) | (cand < KEY_NEG_INF)
        return jnp.where(ok, cand, t)

    def top_bit(p, t):
        cand = t + jnp.left_shift(jnp.int32(1), 31 - p)
        cand_bf = pltpu.bitcast(cand ^ ((cand >> 31) & 0x7FFF0000), F32).astype(jnp.bfloat16)
        return accept(count_top(cand_bf), cand, t)

    def low_bit(p, t):
        cand = t + jnp.left_shift(jnp.int32(1), 16 - p)
        cand_f = _key_to_f32(cand)
        return accept(count(lambda tile, _: tile >= cand_f), cand, t)

    top_key = lax.fori_loop(0, 16, top_bit, jnp.full((1, tq), INT_MIN, I32))
    top_key = jnp.where(top_key == 0, jnp.int32(-65536), top_key)
    thr_key = lax.fori_loop(0, 17, low_bit, top_key)
    thr = jnp.maximum(_key_to_f32(thr_key), jnp.finfo(F32).min)
    tie_budget = n_sel - count(lambda tile, _: tile > thr)

    w_q = _half_masked_weights(sq_ref[...], DSA_HEADS // 2)
    m_ref[...] = jnp.full(m_ref.shape, M_INIT, F32)
    acc_ref[...] = jnp.zeros(acc_ref.shape, F32)
    tie_ref[...] = jnp.zeros(tie_ref.shape, F32)

    def step(j0, width, near):
        row0 = _row_start(j0, tk)
        rows = width * tk
        sc = sc_ref[pl.ds(row0, rows), :]
        tie = sc == thr
        tie_one = jnp.where(tie, 1.0, 0.0).astype(jnp.bfloat16)
        seen = tie_ref[...]
        ranks = []
        for w in range(width):
            ranks.append(_dot(tri_ref[...], tie_one[w * tk:(w + 1) * tk]) + seen)
            seen = ranks[-1][tk - 1:tk, :]
        tie_ref[...] = seen
        sel = (sc > thr) | (tie & (jnp.concatenate(ranks, axis=0) <= tie_budget))
        s_all = _dot_nt(skk_ref[pl.ds(row0, rows), :], w_q)
        parts = []
        for h in range(DSA_HEADS):
            s = s_all[:, h * tq:(h + 1) * tq]
            if near:
                s = s + (bias_ref[h] if width == 2 else bias_ref[h, tk:, :])
            parts.append(jnp.where(sel, s, S_MASKED))
        m_new, alpha, p = _softmax_step(jnp.concatenate(parts, axis=1), m_ref[...])
        vt = jnp.concatenate([svt_ref[j0 + w] for w in range(width)], axis=1)
        acc_ref[...] = alpha * acc_ref[...] + _dot(vt, p)
        m_ref[...] = m_new

    _run_tiles(i, step)

    acc = acc_ref[...]
    o = acc[:HEAD_DIM, :] / acc[HEAD_DIM:HEAD_DIM + 1, :]
    for pr in range(DSA_HEADS // 2):
        pair = jnp.concatenate([o[:, 2 * pr * tq:(2 * pr + 1) * tq], o[:, (2 * pr + 1) * tq:(2 * pr + 2) * tq]],
                               axis=0)
        o_ref[:, pr * LANES:(pr + 1) * LANES] = pair.T.astype(BF16)


def _attn_b(p, bias, vis, *, batch, nq, tq, tkeys, q_tile_index, n_sel):
    tk = p["svt"].shape[2]
    nkt = tkeys // tk
    assert (tkeys + tk) // COUNT_ROWS <= 256, "bf16 partial counts must stay exact"
    kern = functools.partial(_attn_b_kernel, q_tile_index=q_tile_index, n_sel=n_sel)
    tri = jnp.tril(jnp.ones((tk, tk), jnp.bfloat16))
    qrow = lambda b, i: (b * nq + i, 0)
    krow = lambda b, i: (b, 0)
    return pl.pallas_call(
        kern, grid=(batch, nq),
        in_specs=[pl.BlockSpec((tq, IDX_HEADS * IDX_DIM), qrow),
                  pl.BlockSpec((tkeys, LANES), krow),
                  pl.BlockSpec((8, tq), lambda b, i: (0, b * nq + i)),
                  pl.BlockSpec((tq, DSA_WIDTH), qrow),
                  pl.BlockSpec((tkeys, LANES), krow),
                  pl.BlockSpec((nkt, VB_ROWS, tk), lambda b, i: (b, 0, 0)),
                  _const_spec(bias.shape), _const_spec(vis.shape), _const_spec(tri.shape)],
        out_specs=pl.BlockSpec((tq, DSA_WIDTH), qrow),
        out_shape=jax.ShapeDtypeStruct((batch * nq * tq, DSA_WIDTH), BF16),
        scratch_shapes=[pltpu.VMEM((tkeys + tk, tq), F32),
                        pltpu.VMEM((tkeys + tk, tq), jnp.bfloat16),
                        pltpu.VMEM((1, DSA_HEADS * tq), F32),
                        pltpu.VMEM((VB_ROWS, DSA_HEADS * tq), F32),
                        pltpu.VMEM((1, tq), F32)],
        compiler_params=_params(2), name="attn_b")(
            p["iq"], p["ikk"], p["iwt"], p["sq"], p["skk"], p["svt"], bias, vis, tri)


def _t5_bucket(rel):
    half = NUM_BUCKETS // 2
    exact = half // 2
    n = jnp.abs(rel)
    logn = jnp.log(jnp.maximum(n, 1).astype(F32) / exact)
    large = exact + (logn / math.log(MAX_DISTANCE / exact) * (half - exact)).astype(I32)
    large = jnp.minimum(large, half - 1)
    return jnp.where(rel > 0, half, 0) + jnp.where(n < exact, n, large)


def _bias_lookup(rel_bias, bucket):
    out = jnp.zeros((rel_bias.shape[1],) + bucket.shape, F32)
    for b in range(NUM_BUCKETS):
        out = out + jnp.where(bucket[None] == b, rel_bias[b].astype(F32)[:, None, None], 0.0)
    return out


def _bias_tiles(rel_bias, tk, tq, q_valid, k_valid_diag):
    kk = jnp.arange(tk, dtype=I32)[:, None]
    qq = jnp.arange(tq, dtype=I32)[None, :]
    far = _bias_lookup(rel_bias, _t5_bucket(jnp.full((1, 1), -(2 * MAX_DISTANCE), I32)))
    tiles = [(_bias_lookup(rel_bias, _t5_bucket(kk + d - qq)) - far) * LOG2E for d in (-tk, 0)]
    vis = ((kk // CHUNK) <= (qq // CHUNK)) & (kk < k_valid_diag) | (qq >= q_valid)
    tiles[1] = jnp.where(vis[None], tiles[1], M_INIT)
    return jnp.concatenate(tiles, axis=1), vis.astype(F32)


def _token_mix(p, rel_bias, lamv, gsub, *, batch, nq, tq, tkeys, q_tile_index, q_valid, k_valid_diag,
               n_sel, lam_init):
    bias, vis = _bias_tiles(rel_bias, KEY_TILE, tq, q_valid, k_valid_diag)
    gsub_b = jnp.broadcast_to(gsub.astype(F32)[:, None], (DIFF_V_DIM, tq))
    o_a = _attn_a(lamv, p["dq"], p["dk"], p["dvt"], bias[:DIFF_HEADS], gsub_b, batch=batch, nq=nq, tq=tq,
                  tkeys=tkeys, q_tile_index=q_tile_index, lam_init=lam_init)
    o_b = _attn_b(p, bias[DIFF_HEADS:], vis, batch=batch, nq=nq, tq=tq, tkeys=tkeys,
                  q_tile_index=q_tile_index, n_sel=n_sel)
    return o_a, o_b


def _pad_rows(a, rows):
    return jnp.pad(a, ((0, 0), (0, rows - a.shape[1])) + ((0, 0),) * (a.ndim - 2))


def kernel(x_prompt, x_sample, cache_diff_k, cache_diff_v, cache_dsa_k, cache_dsa_v, cache_idx_k, g_ffn1, w1_gate, w1_up, w1_down, g_mix, w_in, g_dq, g_dk, g_sq, g_sk, lam_q1, lam_k1, lam_q2, lam_k2, g_subln, w_out, g_ffn2, w2_gate, w2_up, w2_down, rel_bias):
    depth = g_ffn1.shape[0]
    assert depth == 1, "one layer"
    l = 0
    batch, seq, d_model = x_prompt.shape
    sbatch, sseq, _ = x_sample.shape
    past = cache_diff_k.shape[2]
    tk = KEY_TILE
    assert seq % Q_TILE == 0 and Q_TILE % CHUNK == 0 and Q_TILE == tk
    assert past % tk == 0 and past % CHUNK == 0 and sseq <= CHUNK and sseq % 8 == 0
    lam_init = 0.8 - 0.6 * math.exp(-0.3 * l)

    bf = lambda w: w.astype(BF16)
    row = lambda g: g.astype(F32)[None, :]
    w1 = (row(g_ffn1[l]), bf(w1_gate[l]), bf(w1_up[l]), bf(w1_down[l]))
    w2 = (row(g_ffn2[l]), bf(w2_gate[l]), bf(w2_up[l]), bf(w2_down[l]))
    w_in_p = jnp.pad(bf(w_in[l]), ((0, 0), (0, IN_COLS_PAD - IN_COLS)))
    wo_a, wo_b = bf(w_out[l][:DIFF_WIDTH]), bf(w_out[l][DIFF_WIDTH:])
    lamv = jnp.pad(jnp.stack([lam_q1[l], lam_k1[l], lam_q2[l], lam_k2[l]]).astype(F32), ((0, 4), (0, LANES - HEAD_DIM)))
    proj = functools.partial(_proj, g_mix=row(g_mix[l]), w_in=w_in_p, g_dq=g_dq[l], g_dk=g_dk[l], g_sq=g_sq[l],
                             g_sk=g_sk[l])

    xp1 = _ffn(x_prompt.reshape(batch * seq, d_model), *w1)
    pp = proj(xp1)
    oa_p, ob_p = _token_mix(pp, rel_bias, lamv, g_subln[l], batch=batch, nq=seq // Q_TILE, tq=Q_TILE, tkeys=seq,
                            q_tile_index=None, q_valid=Q_TILE, k_valid_diag=tk,
                            n_sel=min(TOPK_MAX, seq // 4), lam_init=lam_init)
    yp = _out_ffn(xp1, oa_p, ob_p, wo_a, wo_b, *w2).reshape(batch, seq, d_model)

    ns = sbatch * sseq
    xs1 = _ffn(x_sample.reshape(ns, d_model), *w1)
    ps = proj(xs1)
    tq_s = LANES
    keys_s = past + tk
    per_b = lambda a: a.reshape(sbatch, sseq, a.shape[-1])

    def q_rows(a):
        return _pad_rows(per_b(a), tq_s).reshape(sbatch * tq_s, a.shape[-1])

    def k_rows(cache, new):
        both = jnp.concatenate([cache.astype(BF16), per_b(new)], axis=1)
        return _pad_rows(both, keys_s).reshape(sbatch * keys_s, new.shape[-1])

    def v_tiles(cache, new_t, rows, heads):
        rv = rows - ONES_ROWS
        c = cache.astype(BF16).reshape(sbatch, past // tk, tk, heads, rv)
        c = jnp.transpose(c, (0, 1, 3, 4, 2))
        c = jnp.concatenate([c, jnp.ones((sbatch, past // tk, heads, ONES_ROWS, tk), BF16)], axis=3)
        c = c.reshape(sbatch, past // tk, heads * rows, tk)
        n = new_t.reshape(heads * rows, sbatch, sseq)
        n = jnp.pad(jnp.transpose(n, (1, 0, 2)), ((0, 0), (0, 0), (0, tk - sseq)))[:, None]
        return jnp.concatenate([c, n], axis=1).reshape(sbatch * (past // tk + 1), heads * rows, tk)

    dup = lambda c: jnp.concatenate([c, c], axis=-1)
    iwt_s = jnp.pad(ps["iwt"].reshape(8, sbatch, sseq), ((0, 0), (0, 0), (0, tq_s - sseq))).reshape(8, sbatch * tq_s)
    psa = {
        "dq": q_rows(ps["dq"]), "sq": q_rows(ps["sq"]), "iq": q_rows(ps["iq"]), "iwt": iwt_s,
        "dk": k_rows(cache_diff_k[l].reshape(sbatch, past, DIFF_QK), ps["dk"]),
        "skk": k_rows(dup(cache_dsa_k[l]), ps["skk"]),
        "ikk": k_rows(dup(cache_idx_k[l]), ps["ikk"]),
        "dvt": v_tiles(cache_diff_v[l].reshape(sbatch, past, DIFF_WIDTH), ps["dvt"], VA_ROWS, DIFF_HEADS),
        "svt": v_tiles(cache_dsa_v[l], ps["svt"], VB_ROWS, 1),
    }
    oa_s, ob_s = _token_mix(psa, rel_bias, lamv, g_subln[l], batch=sbatch, nq=1, tq=tq_s, tkeys=keys_s,
                            q_tile_index=past // tk, q_valid=sseq, k_valid_diag=sseq,
                            n_sel=min(TOPK_MAX, (past + sseq) // 4), lam_init=lam_init)
    take = lambda o: o.reshape(sbatch, tq_s, o.shape[-1])[:, :sseq].reshape(ns, o.shape[-1])
    ys = _out_ffn(xs1, take(oa_s), take(ob_s), wo_a, wo_b, *w2).reshape(sbatch, sseq, d_model)

    def caches(p, b, t):
        return (p["dk_c"].reshape(depth, b, t, DIFF_HEADS, 2, HEAD_DIM),
                p["dv_c"].reshape(depth, b, t, DIFF_HEADS, DIFF_V_DIM),
                p["sk_c"].reshape(depth, b, t, HEAD_DIM), p["sv_c"].reshape(depth, b, t, HEAD_DIM),
                p["ik_c"].reshape(depth, b, t, IDX_DIM))

    return (yp, ys) + caches(pp, batch, seq) + caches(ps, sbatch, sseq)
```

```python
import functools
import math

import jax
import jax.numpy as jnp
from jax import lax
from jax.experimental import pallas as pl
from jax.experimental.pallas import tpu as pltpu

F32 = jnp.float32
BF16 = jnp.bfloat16
I32 = jnp.int32

CHUNK = 64
HEAD_DIM = 64
DIFF_HEADS = 4
DIFF_V_DIM = 2 * HEAD_DIM
DSA_HEADS = 8
IDX_HEADS = 4
IDX_DIM = 64
TOPK_MAX = 256
NUM_BUCKETS = 32
MAX_DISTANCE = 128
EPS = 1e-6
IDX_SCALE = (IDX_HEADS ** -0.5) * (IDX_DIM ** -0.5)
QK_SCALE = HEAD_DIM ** -0.5
LOG2E = math.log2(math.e)

DIFF_QK = DIFF_HEADS * 2 * HEAD_DIM
DIFF_WIDTH = DIFF_HEADS * DIFF_V_DIM
DSA_WIDTH = DSA_HEADS * HEAD_DIM
IN_COLS = 3 * DIFF_QK + DSA_WIDTH + 2 * HEAD_DIM + IDX_HEADS * IDX_DIM + IDX_DIM + IDX_HEADS

LANES = 128
BF16_SUBLANES = 16
MXU_COLS = 256
VMEM_LIMIT_BYTES = 60000 * 1024

IN_COLS_PAD = -(-IN_COLS // MXU_COLS) * MXU_COLS
ONES_ROWS = BF16_SUBLANES
VA_ROWS = DIFF_V_DIM + ONES_ROWS
VB_ROWS = HEAD_DIM + ONES_ROWS

KEY_TILE = 256
Q_TILE = 256
M_INIT = -(2.0 ** 100)
S_MASKED = -2e30
INT_MIN = -(2 ** 31)
KEY_NEG_INF = INT_MIN + 0x7FFFFF
COUNT_ROWS = 32
HEAD_GROUPS = 2

_C_DQ, _C_DK, _C_DV, _C_SQ = 0, DIFF_QK, 2 * DIFF_QK, 3 * DIFF_QK
_C_SKV = _C_SQ + DSA_WIDTH
_C_IQ = _C_SKV + 2 * HEAD_DIM
_C_IKW = _C_IQ + IDX_HEADS * IDX_DIM


def _params(n_axes):
    return pltpu.CompilerParams(dimension_semantics=("arbitrary",) * n_axes,
                                vmem_limit_bytes=VMEM_LIMIT_BYTES)


def _const_spec(shape):
    nd = len(shape)
    return pl.BlockSpec(shape, lambda *_: (0,) * nd, pipeline_mode=pl.Buffered(1))


def _rms(x, g):
    return x * lax.rsqrt(jnp.mean(x * x, axis=-1, keepdims=True) + EPS) * g


def _dot(a, b):
    return jnp.dot(a, b, preferred_element_type=F32)


def _dot_nt(a, b):
    return lax.dot_general(a, b, (((1,), (1,)), ((), ())), preferred_element_type=F32)


def _ff_chunks(d_ff):
    step = 4 * MXU_COLS
    return [(s, min(step, d_ff - s)) for s in range(0, d_ff, step)]


def _swiglu_half(x, g_ref, wg_ref, wu_ref, wd_ref):
    h = _rms(x, g_ref[...]).astype(BF16)
    acc = None
    for start, width in _ff_chunks(wg_ref.shape[1]):
        gate = _dot(h, wg_ref[:, start:start + width])
        up = _dot(h, wu_ref[:, start:start + width])
        act = (gate * (1.0 / (1.0 + jnp.exp(-gate))) * up).astype(BF16)
        part = _dot(act, wd_ref[start:start + width, :])
        acc = part if acc is None else acc + part
    return x + 0.5 * acc


def _ffn_kernel(x_ref, g_ref, wg_ref, wu_ref, wd_ref, o_ref):
    o_ref[...] = _swiglu_half(x_ref[...], g_ref, wg_ref, wu_ref, wd_ref)


def _out_ffn_kernel(x_ref, oa_ref, ob_ref, woa_ref, wob_ref, g_ref, wg_ref, wu_ref, wd_ref, o_ref):
    x = x_ref[...] + _dot(oa_ref[...], woa_ref[...]) + _dot(ob_ref[...], wob_ref[...])
    o_ref[...] = _swiglu_half(x, g_ref, wg_ref, wu_ref, wd_ref)


def _row_tile(n):
    for tm in (512, 256, 128, 64, 32, 16, 8):
        if n % tm == 0:
            return tm
    raise ValueError(f"row count {n} is not a multiple of 8")


def _ffn(x, g, wg, wu, wd):
    n, d = x.shape
    tm = _row_tile(n)
    row = pl.BlockSpec((tm, d), lambda i: (i, 0))
    return pl.pallas_call(
        _ffn_kernel, grid=(n // tm,),
        in_specs=[row, _const_spec(g.shape), _const_spec(wg.shape), _const_spec(wu.shape), _const_spec(wd.shape)],
        out_specs=row, out_shape=jax.ShapeDtypeStruct((n, d), F32),
        compiler_params=_params(1), name="ffn")(x, g, wg, wu, wd)


def _out_ffn(x, oa, ob, woa, wob, g, wg, wu, wd):
    n, d = x.shape
    tm = _row_tile(n)
    row = pl.BlockSpec((tm, d), lambda i: (i, 0))
    half = pl.BlockSpec((tm, oa.shape[1]), lambda i: (i, 0))
    consts = [woa, wob, g, wg, wu, wd]
    return pl.pallas_call(
        _out_ffn_kernel, grid=(n // tm,),
        in_specs=[row, half, half] + [_const_spec(c.shape) for c in consts],
        out_specs=row, out_shape=jax.ShapeDtypeStruct((n, d), F32),
        compiler_params=_params(1), name="out_ffn")(x, oa, ob, *consts)


def _dup_halves(t, first):
    lane = lax.broadcasted_iota(I32, t.shape, 1)
    rolled = pltpu.roll(t, HEAD_DIM, axis=1)
    if first:
        return jnp.where(lane < HEAD_DIM, t, rolled)
    return jnp.where(lane < HEAD_DIM, rolled, t)


def _proj_kernel(x_ref, g_ref, win_ref, gdq_ref, gdk_ref, gsq_ref, gsk_ref, g512_ref, g128_ref,
                 dkc_ref, dvc_ref, skc_ref, svc_ref, ikc_ref,
                 dq_ref, dkb_ref, dvt_ref, sq_ref, skk_ref, svt_ref, iq_ref, ikk_ref, iwt_ref):
    h = _rms(x_ref[...], g_ref[...]).astype(BF16)
    z = _dot(h, win_ref[...])
    n_sub, _, t_sub = dvt_ref.shape

    def headnorm(seg, ones_ref, gain):
        ss = _dot((seg * seg).astype(BF16), ones_ref[...])
        return seg * lax.rsqrt(ss * (1.0 / HEAD_DIM) + EPS) * gain

    dq_ref[...] = headnorm(z[:, _C_DQ:_C_DQ + DIFF_QK], g512_ref, gdq_ref[...]).astype(BF16)
    dk = headnorm(z[:, _C_DK:_C_DK + DIFF_QK], g512_ref, gdk_ref[...])
    dkc_ref[...] = dk
    dkb_ref[...] = dk.astype(BF16)
    sq_ref[...] = headnorm(z[:, _C_SQ:_C_SQ + DSA_WIDTH], g512_ref, gsq_ref[...]).astype(BF16)

    dv = z[:, _C_DV:_C_DV + DIFF_WIDTH]
    dvc_ref[...] = dv
    ones_a = jnp.ones((ONES_ROWS, t_sub), BF16)
    for s in range(n_sub):
        for hh in range(DIFF_HEADS):
            blk = dv[s * t_sub:(s + 1) * t_sub, hh * DIFF_V_DIM:(hh + 1) * DIFF_V_DIM]
            dvt_ref[s, hh * VA_ROWS:hh * VA_ROWS + DIFF_V_DIM, :] = blk.T.astype(BF16)
            dvt_ref[s, hh * VA_ROWS + DIFF_V_DIM:(hh + 1) * VA_ROWS, :] = ones_a

    skv = z[:, _C_SKV:_C_SKV + LANES]
    skn = headnorm(skv, g128_ref, gsk_ref[...])
    skk = _dup_halves(skn, first=True)
    svv = _dup_halves(skv, first=False)
    skc_ref[...] = skk[:, :HEAD_DIM]
    svc_ref[...] = svv[:, :HEAD_DIM]
    skk_ref[...] = skk.astype(BF16)
    for s in range(n_sub):
        svt_ref[s, 0:HEAD_DIM, :] = svv[s * t_sub:(s + 1) * t_sub, :].T[:HEAD_DIM, :].astype(BF16)
        svt_ref[s, HEAD_DIM:VB_ROWS, :] = ones_a

    iq_ref[...] = z[:, _C_IQ:_C_IQ + IDX_HEADS * IDX_DIM].astype(BF16)
    ikw = z[:, _C_IKW:_C_IKW + LANES]
    ikk = _dup_halves(ikw, first=True)
    ikc_ref[...] = ikk[:, :HEAD_DIM]
    ikk_ref[...] = ikk.astype(BF16)
    iwt_ref[...] = ikw.T[HEAD_DIM:HEAD_DIM + 8, :] * IDX_SCALE


def _proj(x1, g_mix, w_in, g_dq, g_dk, g_sq, g_sk):
    n, d = x1.shape
    tm = _row_tile(n)
    t_sub = min(KEY_TILE, tm)
    n_sub = tm // t_sub
    nt = n // t_sub

    def tile8(g, scale):
        return (jnp.tile(g.astype(F32), DIFF_QK // HEAD_DIM) * scale)[None, :]

    gsk_pad = jnp.concatenate([g_sk.astype(F32), jnp.ones((HEAD_DIM,), F32)])[None, :]
    grp = jnp.arange(DIFF_QK) // HEAD_DIM
    ones512 = (grp[:, None] == grp[None, :]).astype(BF16)
    ones128 = ones512[:LANES, :LANES]
    consts = [g_mix, w_in, tile8(g_dq, QK_SCALE * LOG2E), tile8(g_dk, 1.0), tile8(g_sq, QK_SCALE * LOG2E), gsk_pad,
              ones512, ones128]

    def rows(width):
        return pl.BlockSpec((tm, width), lambda i: (i, 0))

    def tposed(nrows):
        return pl.BlockSpec((n_sub, nrows, t_sub), lambda i: (i, 0, 0))

    out_shapes = [
        jax.ShapeDtypeStruct((n, DIFF_QK), F32), jax.ShapeDtypeStruct((n, DIFF_WIDTH), F32),
        jax.ShapeDtypeStruct((n, HEAD_DIM), F32), jax.ShapeDtypeStruct((n, HEAD_DIM), F32),
        jax.ShapeDtypeStruct((n, IDX_DIM), F32),
        jax.ShapeDtypeStruct((n, DIFF_QK), BF16), jax.ShapeDtypeStruct((n, DIFF_QK), BF16),
        jax.ShapeDtypeStruct((nt, DIFF_HEADS * VA_ROWS, t_sub), BF16),
        jax.ShapeDtypeStruct((n, DSA_WIDTH), BF16), jax.ShapeDtypeStruct((n, LANES), BF16),
        jax.ShapeDtypeStruct((nt, VB_ROWS, t_sub), BF16),
        jax.ShapeDtypeStruct((n, IDX_HEADS * IDX_DIM), BF16), jax.ShapeDtypeStruct((n, LANES), BF16),
        jax.ShapeDtypeStruct((8, n), F32),
    ]
    out_specs = [rows(DIFF_QK), rows(DIFF_WIDTH), rows(HEAD_DIM), rows(HEAD_DIM), rows(IDX_DIM),
                 rows(DIFF_QK), rows(DIFF_QK), tposed(DIFF_HEADS * VA_ROWS),
                 rows(DSA_WIDTH), rows(LANES), tposed(VB_ROWS),
                 rows(IDX_HEADS * IDX_DIM), rows(LANES), pl.BlockSpec((8, tm), lambda i: (0, i))]
    outs = pl.pallas_call(
        _proj_kernel, grid=(n // tm,),
        in_specs=[rows(d)] + [_const_spec(c.shape) for c in consts],
        out_specs=out_specs, out_shape=out_shapes,
        compiler_params=_params(1), name="proj")(x1, *consts)
    names = ("dk_c", "dv_c", "sk_c", "sv_c", "ik_c", "dq", "dk", "dvt", "sq", "skk", "svt", "iq", "ikk", "iwt")
    return dict(zip(names, outs))


def _half_masked_weights(q_tile, n_pairs):
    lane = lax.broadcasted_iota(I32, (q_tile.shape[0], LANES), 1)
    parts = []
    for pr in range(n_pairs):
        qp = q_tile[:, pr * LANES:(pr + 1) * LANES].astype(F32)
        parts.append(jnp.where(lane < HEAD_DIM, qp, 0.0).astype(q_tile.dtype))
        parts.append(jnp.where(lane >= HEAD_DIM, qp, 0.0).astype(q_tile.dtype))
    return jnp.concatenate(parts, axis=0)


def _softmax_step(s, m_old):
    m_new = jnp.maximum(m_old, jnp.max(s, axis=0, keepdims=True).astype(F32))
    alpha = jnp.exp2(m_old - m_new)
    p = jnp.exp2(s - m_new.astype(s.dtype))
    return m_new, alpha, p


def _run_tiles(i, step):
    def far_pair(p, carry):
        step(2 * p, 2, False)
        return carry

    if isinstance(i, int):
        n_far = max(i - 1, 0)
        lax.fori_loop(0, n_far // 2, far_pair, 0)
        if n_far % 2:
            step(n_far - 1, 1, False)
        if i >= 1:
            step(i - 1, 2, True)
        else:
            step(0, 1, True)
    else:
        n_far = jnp.maximum(i - 1, 0)
        lax.fori_loop(0, lax.shift_right_logical(n_far, 1), far_pair, 0)
        pl.when((n_far & 1) == 1)(lambda: step(n_far - 1, 1, False))
        pl.when(i >= 1)(lambda: step(i - 1, 2, True))
        pl.when(i == 0)(lambda: step(0, 1, True))


def _row_start(j, tk):
    return j * tk if isinstance(j, int) else pl.multiple_of(j * tk, tk)


def _attn_a_kernel(lamv_ref, q_ref, k_ref, vt_ref, bias_ref, gsub_ref, o_ref, m_ref, acc_ref,
                   *, q_tile_index, lam_init):
    tq = q_ref.shape[0]
    tk = vt_ref.shape[2]
    i = pl.program_id(1) if q_tile_index is None else q_tile_index
    weights = _half_masked_weights(q_ref[...], DIFF_HEADS)
    m_ref[...] = jnp.full(m_ref.shape, M_INIT, F32)
    acc_ref[...] = jnp.zeros(acc_ref.shape, F32)

    def step(j0, width, near):
        row0 = _row_start(j0, tk)
        rows = width * tk
        scores = [_dot_nt(k_ref[pl.ds(row0, rows), h * LANES:(h + 1) * LANES],
                          weights[2 * h * tq:(2 * h + 2) * tq, :])
                  for h in range(DIFF_HEADS)]
        for h in range(DIFF_HEADS):
            s = scores[h]
            if near:
                b = bias_ref[h] if width == 2 else bias_ref[h, tk:, :]
                s = s + jnp.concatenate([b, b], axis=1)
            m_new, alpha, p = _softmax_step(s.astype(BF16), m_ref[h])
            m_ref[h] = m_new
            vt = jnp.concatenate([vt_ref[j0 + w, h * VA_ROWS:(h + 1) * VA_ROWS, :] for w in range(width)], axis=1)
            acc_ref[h] = alpha * acc_ref[h] + _dot(vt, p)

    _run_tiles(i, step)

    lv = lamv_ref[...]
    lam = (jnp.exp(jnp.sum(lv[0:1] * lv[1:2], axis=1, keepdims=True))
           - jnp.exp(jnp.sum(lv[2:3] * lv[3:4], axis=1, keepdims=True)) + lam_init)
    for h in range(DIFF_HEADS):
        acc = acc_ref[h]
        o = acc[:DIFF_V_DIM, :] / acc[DIFF_V_DIM:DIFF_V_DIM + 1, :]
        d = o[:, :tq] - lam * o[:, tq:]
        y = d * lax.rsqrt(jnp.mean(d * d, axis=0, keepdims=True) + EPS) * gsub_ref[...] * (1.0 - lam_init)
        o_ref[:, h * DIFF_V_DIM:(h + 1) * DIFF_V_DIM] = y.T.astype(BF16)


def _attn_a(lamv, q, k, vt, bias, gsub, *, batch, nq, tq, tkeys, q_tile_index, lam_init):
    tk = vt.shape[2]
    nkt = tkeys // tk
    kern = functools.partial(_attn_a_kernel, q_tile_index=q_tile_index, lam_init=lam_init)
    return pl.pallas_call(
        kern, grid=(batch, nq),
        in_specs=[_const_spec(lamv.shape),
                  pl.BlockSpec((tq, DIFF_QK), lambda b, i: (b * nq + i, 0)),
                  pl.BlockSpec((tkeys, DIFF_QK), lambda b, i: (b, 0)),
                  pl.BlockSpec((nkt, DIFF_HEADS * VA_ROWS, tk), lambda b, i: (b, 0, 0)),
                  _const_spec(bias.shape), _const_spec(gsub.shape)],
        out_specs=pl.BlockSpec((tq, DIFF_WIDTH), lambda b, i: (b * nq + i, 0)),
        out_shape=jax.ShapeDtypeStruct((batch * nq * tq, DIFF_WIDTH), BF16),
        scratch_shapes=[pltpu.VMEM((DIFF_HEADS, 1, 2 * tq), F32),
                        pltpu.VMEM((DIFF_HEADS, VA_ROWS, 2 * tq), F32)],
        compiler_params=_params(2), name="attn_a")(lamv, q, k, vt, bias, gsub)


def _key_to_f32(key):
    return pltpu.bitcast(key ^ ((key >> 31) & 0x7FFFFFFF), F32)


def _attn_b_kernel(iq_ref, ikk_ref, iwt_ref, sq_ref, skk_ref, svt_ref, bias_ref, vis_ref, tri_ref, o_ref,
                   sc_ref, hi_ref, m_ref, acc_ref, tie_ref, *, q_tile_index, n_sel):
    tq = iq_ref.shape[0]
    tk = svt_ref.shape[2]
    static = q_tile_index is not None
    i = q_tile_index if static else pl.program_id(1)
    n_tiles = i + 1
    n_pairs = (n_tiles + 1) // 2 if static else lax.shift_right_logical(n_tiles + 1, 1)
    neg_inf = float("-inf")

    w_idx = _half_masked_weights(iq_ref[...], IDX_HEADS // 2)
    iw = iwt_ref[...]

    def index_scores(j, visible):
        row0 = _row_start(j, tk)
        rel = _dot_nt(ikk_ref[pl.ds(row0, tk), :], w_idx)
        score = None
        for h in range(IDX_HEADS):
            term = iw[h:h + 1, :] * jnp.maximum(rel[:, h * tq:(h + 1) * tq], 0.0)
            score = term if score is None else score + term
        if visible is not None:
            score = jnp.where(visible > 0.0, score, neg_inf)
        sc_ref[pl.ds(row0, tk), :] = score
        top = pltpu.bitcast(pltpu.bitcast(score, I32) & jnp.int32(-65536), F32)
        hi_ref[pl.ds(row0, tk), :] = top.astype(jnp.bfloat16)

    def idx_body(j, carry):
        index_scores(j, None)
        return carry

    lax.fori_loop(0, i, idx_body, 0)
    index_scores(i, vis_ref[...])
    pad0 = _row_start(n_tiles, tk)
    sc_ref[pl.ds(pad0, tk), :] = jnp.full((tk, tq), neg_inf, F32)
    hi_ref[pl.ds(pad0, tk), :] = jnp.full((tk, tq), neg_inf, jnp.bfloat16)

    n_slab = 2 * tk // COUNT_ROWS

    def count(pred):
        def body(pp, acc):
            row0 = pl.multiple_of(pp * (2 * tk), 2 * tk)
            hit = pred(sc_ref[pl.ds(row0, 2 * tk), :], row0).astype(F32)
            return acc + jnp.sum(hit.reshape(n_slab, COUNT_ROWS, tq), axis=0)
        acc = lax.fori_loop(0, n_pairs, body, jnp.zeros((COUNT_ROWS, tq), F32))
        return jnp.sum(acc, axis=0, keepdims=True)

    def count_top(cand_bf):
        one, zero = jnp.ones((), jnp.bfloat16), jnp.zeros((), jnp.bfloat16)

        def body(pp, acc):
            row0 = pl.multiple_of(pp * (2 * tk), 2 * tk)
            hit = jnp.where(hi_ref[pl.ds(row0, 2 * tk), :] >= cand_bf, one, zero)
            slabs = [hit[r * COUNT_ROWS:(r + 1) * COUNT_ROWS] for r in range(n_slab)]
            while len(slabs) > 1:
                slabs = [a + b for a, b in zip(slabs[0::2], slabs[1::2])]
            return acc + slabs[0]
        acc = lax.fori_loop(0, n_pairs, body, jnp.zeros((COUNT_ROWS, tq), jnp.bfloat16))
        return jnp.sum(acc.astype(F32), axis=0, keepdims=True)

    def accept(cnt, cand, t):
        ok = (cnt >= n_sel) | (cand < KEY_NEG_INF)
        return jnp.where(ok, cand, t)

    def top_bit(p, t):
        cand = t + jnp.left_shift(jnp.int32(1), 31 - p)
        cand_bf = pltpu.bitcast(cand ^ ((cand >> 31) & 0x7FFF0000), F32).astype(jnp.bfloat16)
        return accept(count_top(cand_bf), cand, t)

    def low_bit(p, t):
        cand = t + jnp.left_shift(jnp.int32(1), 16 - p)
        cand_f = _key_to_f32(cand)
        return accept(count(lambda tile, _: tile >= cand_f), cand, t)

    top_key = lax.fori_loop(0, 16, top_bit, jnp.full((1, tq), INT_MIN, I32))
    top_key = jnp.where(top_key == 0, jnp.int32(-65536), top_key)
    thr_key = lax.fori_loop(0, 17, low_bit, top_key)
    thr = jnp.maximum(_key_to_f32(thr_key), jnp.finfo(F32).min)
    tie_budget = n_sel - count(lambda tile, _: tile > thr)

    w_q = _half_masked_weights(sq_ref[...], DSA_HEADS // 2)
    m_ref[...] = jnp.full(m_ref.shape, M_INIT, F32)
    acc_ref[...] = jnp.zeros(acc_ref.shape, F32)
    tie_ref[...] = jnp.zeros(tie_ref.shape, F32)
    grp = DSA_HEADS // HEAD_GROUPS

    def step(j0, width, near):
        row0 = _row_start(j0, tk)
        rows = width * tk
        sc = sc_ref[pl.ds(row0, rows), :]
        tie = sc == thr
        tie_one = jnp.where(tie, 1.0, 0.0).astype(jnp.bfloat16)
        seen = tie_ref[...]
        ranks = []
        for w in range(width):
            ranks.append(_dot(tri_ref[...], tie_one[w * tk:(w + 1) * tk]) + seen)
            seen = ranks[-1][tk - 1:tk, :]
        tie_ref[...] = seen
        sel = (sc > thr) | (tie & (jnp.concatenate(ranks, axis=0) <= tie_budget))
        keys = skk_ref[pl.ds(row0, rows), :]
        scores = [_dot_nt(keys, w_q[g * grp * tq:(g + 1) * grp * tq, :]) for g in range(HEAD_GROUPS)]
        masked = jnp.where(sel, 0.0, S_MASKED).astype(BF16)
        vt = jnp.concatenate([svt_ref[j0 + w] for w in range(width)], axis=1)
        for g in range(HEAD_GROUPS):
            parts = []
            for hh in range(grp):
                s = scores[g][:, hh * tq:(hh + 1) * tq]
                if near:
                    h = g * grp + hh
                    s = s + (bias_ref[h] if width == 2 else bias_ref[h, tk:, :])
                parts.append(s.astype(BF16) + masked)
            cols = slice(g * grp * tq, (g + 1) * grp * tq)
            m_new, alpha, p = _softmax_step(jnp.concatenate(parts, axis=1), m_ref[:, cols])
            acc_ref[:, cols] = alpha * acc_ref[:, cols] + _dot(vt, p)
            m_ref[:, cols] = m_new

    _run_tiles(i, step)

    acc = acc_ref[...]
    o = acc[:HEAD_DIM, :] / acc[HEAD_DIM:HEAD_DIM + 1, :]
    for pr in range(DSA_HEADS // 2):
        pair = jnp.concatenate([o[:, 2 * pr * tq:(2 * pr + 1) * tq], o[:, (2 * pr + 1) * tq:(2 * pr + 2) * tq]],
                               axis=0)
        o_ref[:, pr * LANES:(pr + 1) * LANES] = pair.T.astype(BF16)


def _attn_b(p, bias, vis, *, batch, nq, tq, tkeys, q_tile_index, n_sel):
    tk = p["svt"].shape[2]
    nkt = tkeys // tk
    assert (tkeys + tk) // COUNT_ROWS <= 256, "bf16 partial counts must stay exact"
    kern = functools.partial(_attn_b_kernel, q_tile_index=q_tile_index, n_sel=n_sel)
    tri = jnp.tril(jnp.ones((tk, tk), jnp.bfloat16))
    qrow = lambda b, i: (b * nq + i, 0)
    krow = lambda b, i: (b, 0)
    return pl.pallas_call(
        kern, grid=(batch, nq),
        in_specs=[pl.BlockSpec((tq, IDX_HEADS * IDX_DIM), qrow),
                  pl.BlockSpec((tkeys, LANES), krow),
                  pl.BlockSpec((8, tq), lambda b, i: (0, b * nq + i)),
                  pl.BlockSpec((tq, DSA_WIDTH), qrow),
                  pl.BlockSpec((tkeys, LANES), krow),
                  pl.BlockSpec((nkt, VB_ROWS, tk), lambda b, i: (b, 0, 0)),
                  _const_spec(bias.shape), _const_spec(vis.shape), _const_spec(tri.shape)],
        out_specs=pl.BlockSpec((tq, DSA_WIDTH), qrow),
        out_shape=jax.ShapeDtypeStruct((batch * nq * tq, DSA_WIDTH), BF16),
        scratch_shapes=[pltpu.VMEM((tkeys + tk, tq), F32),
                        pltpu.VMEM((tkeys + tk, tq), jnp.bfloat16),
                        pltpu.VMEM((1, DSA_HEADS * tq), F32),
                        pltpu.VMEM((VB_ROWS, DSA_HEADS * tq), F32),
                        pltpu.VMEM((1, tq), F32)],
        compiler_params=_params(2), name="attn_b")(
            p["iq"], p["ikk"], p["iwt"], p["sq"], p["skk"], p["svt"], bias, vis, tri)


def _t5_bucket(rel):
    half = NUM_BUCKETS // 2
    exact = half // 2
    n = jnp.abs(rel)
    logn = jnp.log(jnp.maximum(n, 1).astype(F32) / exact)
    large = exact + (logn / math.log(MAX_DISTANCE / exact) * (half - exact)).astype(I32)
    large = jnp.minimum(large, half - 1)
    return jnp.where(rel > 0, half, 0) + jnp.where(n < exact, n, large)


def _bias_lookup(rel_bias, bucket):
    out = jnp.zeros((rel_bias.shape[1],) + bucket.shape, F32)
    for b in range(NUM_BUCKETS):
        out = out + jnp.where(bucket[None] == b, rel_bias[b].astype(F32)[:, None, None], 0.0)
    return out


def _bias_tiles(rel_bias, tk, tq, q_valid, k_valid_diag):
    kk = jnp.arange(tk, dtype=I32)[:, None]
    qq = jnp.arange(tq, dtype=I32)[None, :]
    far = _bias_lookup(rel_bias, _t5_bucket(jnp.full((1, 1), -(2 * MAX_DISTANCE), I32)))
    tiles = [(_bias_lookup(rel_bias, _t5_bucket(kk + d - qq)) - far) * LOG2E for d in (-tk, 0)]
    vis = ((kk // CHUNK) <= (qq // CHUNK)) & (kk < k_valid_diag) | (qq >= q_valid)
    tiles[1] = jnp.where(vis[None], tiles[1], M_INIT)
    return jnp.concatenate(tiles, axis=1), vis.astype(F32)


def _token_mix(p, rel_bias, lamv, gsub, *, batch, nq, tq, tkeys, q_tile_index, q_valid, k_valid_diag,
               n_sel, lam_init):
    bias, vis = _bias_tiles(rel_bias, KEY_TILE, tq, q_valid, k_valid_diag)
    gsub_b = jnp.broadcast_to(gsub.astype(F32)[:, None], (DIFF_V_DIM, tq))
    o_a = _attn_a(lamv, p["dq"], p["dk"], p["dvt"], bias[:DIFF_HEADS], gsub_b, batch=batch, nq=nq, tq=tq,
                  tkeys=tkeys, q_tile_index=q_tile_index, lam_init=lam_init)
    o_b = _attn_b(p, bias[DIFF_HEADS:], vis, batch=batch, nq=nq, tq=tq, tkeys=tkeys,
                  q_tile_index=q_tile_index, n_sel=n_sel)
    return o_a, o_b


def _pad_rows(a, rows):
    return jnp.pad(a, ((0, 0), (0, rows - a.shape[1])) + ((0, 0),) * (a.ndim - 2))


def kernel(x_prompt, x_sample, cache_diff_k, cache_diff_v, cache_dsa_k, cache_dsa_v, cache_idx_k, g_ffn1, w1_gate, w1_up, w1_down, g_mix, w_in, g_dq, g_dk, g_sq, g_sk, lam_q1, lam_k1, lam_q2, lam_k2, g_subln, w_out, g_ffn2, w2_gate, w2_up, w2_down, rel_bias):
    depth = g_ffn1.shape[0]
    assert depth == 1, "one layer"
    l = 0
    batch, seq, d_model = x_prompt.shape
    sbatch, sseq, _ = x_sample.shape
    past = cache_diff_k.shape[2]
    tk = KEY_TILE
    assert seq % Q_TILE == 0 and Q_TILE % CHUNK == 0 and Q_TILE == tk
    assert past % tk == 0 and past % CHUNK == 0 and sseq <= CHUNK and sseq % 8 == 0
    lam_init = 0.8 - 0.6 * math.exp(-0.3 * l)

    bf = lambda w: w.astype(BF16)
    row = lambda g: g.astype(F32)[None, :]
    w1 = (row(g_ffn1[l]), bf(w1_gate[l]), bf(w1_up[l]), bf(w1_down[l]))
    w2 = (row(g_ffn2[l]), bf(w2_gate[l]), bf(w2_up[l]), bf(w2_down[l]))
    w_in_p = jnp.pad(bf(w_in[l]), ((0, 0), (0, IN_COLS_PAD - IN_COLS)))
    wo_a, wo_b = bf(w_out[l][:DIFF_WIDTH]), bf(w_out[l][DIFF_WIDTH:])
    lamv = jnp.pad(jnp.stack([lam_q1[l], lam_k1[l], lam_q2[l], lam_k2[l]]).astype(F32), ((0, 4), (0, LANES - HEAD_DIM)))
    proj = functools.partial(_proj, g_mix=row(g_mix[l]), w_in=w_in_p, g_dq=g_dq[l], g_dk=g_dk[l], g_sq=g_sq[l],
                             g_sk=g_sk[l])

    xp1 = _ffn(x_prompt.reshape(batch * seq, d_model), *w1)
    pp = proj(xp1)
    oa_p, ob_p = _token_mix(pp, rel_bias, lamv, g_subln[l], batch=batch, nq=seq // Q_TILE, tq=Q_TILE, tkeys=seq,
                            q_tile_index=None, q_valid=Q_TILE, k_valid_diag=tk,
                            n_sel=min(TOPK_MAX, seq // 4), lam_init=lam_init)
    yp = _out_ffn(xp1, oa_p, ob_p, wo_a, wo_b, *w2).reshape(batch, seq, d_model)

    ns = sbatch * sseq
    xs1 = _ffn(x_sample.reshape(ns, d_model), *w1)
    ps = proj(xs1)
    tq_s = LANES
    keys_s = past + tk
    per_b = lambda a: a.reshape(sbatch, sseq, a.shape[-1])

    def q_rows(a):
        return _pad_rows(per_b(a), tq_s).reshape(sbatch * tq_s, a.shape[-1])

    def k_rows(cache, new):
        both = jnp.concatenate([cache.astype(BF16), per_b(new)], axis=1)
        return _pad_rows(both, keys_s).reshape(sbatch * keys_s, new.shape[-1])

    def v_tiles(cache, new_t, rows, heads):
        rv = rows - ONES_ROWS
        c = cache.astype(BF16).reshape(sbatch, past // tk, tk, heads, rv)
        c = jnp.transpose(c, (0, 1, 3, 4, 2))
        c = jnp.concatenate([c, jnp.ones((sbatch, past // tk, heads, ONES_ROWS, tk), BF16)], axis=3)
        c = c.reshape(sbatch, past // tk, heads * rows, tk)
        n = new_t.reshape(heads * rows, sbatch, sseq)
        n = jnp.pad(jnp.transpose(n, (1, 0, 2)), ((0, 0), (0, 0), (0, tk - sseq)))[:, None]
        return jnp.concatenate([c, n], axis=1).reshape(sbatch * (past // tk + 1), heads * rows, tk)

    dup = lambda c: jnp.concatenate([c, c], axis=-1)
    iwt_s = jnp.pad(ps["iwt"].reshape(8, sbatch, sseq), ((0, 0), (0, 0), (0, tq_s - sseq))).reshape(8, sbatch * tq_s)
    psa = {
        "dq": q_rows(ps["dq"]), "sq": q_rows(ps["sq"]), "iq": q_rows(ps["iq"]), "iwt": iwt_s,
        "dk": k_rows(cache_diff_k[l].reshape(sbatch, past, DIFF_QK), ps["dk"]),
        "skk": k_rows(dup(cache_dsa_k[l]), ps["skk"]),
        "ikk": k_rows(dup(cache_idx_k[l]), ps["ikk"]),
        "dvt": v_tiles(cache_diff_v[l].reshape(sbatch, past, DIFF_WIDTH), ps["dvt"], VA_ROWS, DIFF_HEADS),
        "svt": v_tiles(cache_dsa_v[l], ps["svt"], VB_ROWS, 1),
    }
    oa_s, ob_s = _token_mix(psa, rel_bias, lamv, g_subln[l], batch=sbatch, nq=1, tq=tq_s, tkeys=keys_s,
                            q_tile_index=past // tk, q_valid=sseq, k_valid_diag=sseq,
                            n_sel=min(TOPK_MAX, (past + sseq) // 4), lam_init=lam_init)
    take = lambda o: o.reshape(sbatch, tq_s, o.shape[-1])[:, :sseq].reshape(ns, o.shape[-1])
    ys = _out_ffn(xs1, take(oa_s), take(ob_s), wo_a, wo_b, *w2).reshape(sbatch, sseq, d_model)

    def caches(p, b, t):
        return (p["dk_c"].reshape(depth, b, t, DIFF_HEADS, 2, HEAD_DIM),
                p["dv_c"].reshape(depth, b, t, DIFF_HEADS, DIFF_V_DIM),
                p["sk_c"].reshape(depth, b, t, HEAD_DIM), p["sv_c"].reshape(depth, b, t, HEAD_DIM),
                p["ik_c"].reshape(depth, b, t, IDX_DIM))

    return (yp, ys) + caches(pp, batch, seq) + caches(ps, sbatch, sseq)
```

```python
import functools
import math

import jax
import jax.numpy as jnp
from jax import lax
from jax.experimental import pallas as pl
from jax.experimental.pallas import tpu as pltpu

F32 = jnp.float32
BF16 = jnp.bfloat16
I32 = jnp.int32

CHUNK = 64
HEAD_DIM = 64
DIFF_HEADS = 4
DIFF_V_DIM = 2 * HEAD_DIM
DSA_HEADS = 8
IDX_HEADS = 4
IDX_DIM = 64
TOPK_MAX = 256
NUM_BUCKETS = 32
MAX_DISTANCE = 128
EPS = 1e-6
IDX_SCALE = (IDX_HEADS ** -0.5) * (IDX_DIM ** -0.5)
QK_SCALE = HEAD_DIM ** -0.5
LOG2E = math.log2(math.e)

DIFF_QK = DIFF_HEADS * 2 * HEAD_DIM
DIFF_WIDTH = DIFF_HEADS * DIFF_V_DIM
DSA_WIDTH = DSA_HEADS * HEAD_DIM
IN_COLS = 3 * DIFF_QK + DSA_WIDTH + 2 * HEAD_DIM + IDX_HEADS * IDX_DIM + IDX_DIM + IDX_HEADS

LANES = 128
BF16_SUBLANES = 16
MXU_COLS = 256
VMEM_LIMIT_BYTES = 60000 * 1024

IN_COLS_PAD = -(-IN_COLS // MXU_COLS) * MXU_COLS
ONES_ROWS = BF16_SUBLANES
VA_ROWS = DIFF_V_DIM + ONES_ROWS
VB_ROWS = HEAD_DIM + ONES_ROWS

KEY_TILE = 256
Q_TILE = 256
M_INIT = -(2.0 ** 100)
S_MASKED = -2e30
INT_MIN = -(2 ** 31)
KEY_NEG_INF = INT_MIN + 0x7FFFFF
COUNT_ROWS = 32

_C_DQ, _C_DK, _C_DV, _C_SQ = 0, DIFF_QK, 2 * DIFF_QK, 3 * DIFF_QK
_C_SKV = _C_SQ + DSA_WIDTH
_C_IQ = _C_SKV + 2 * HEAD_DIM
_C_IKW = _C_IQ + IDX_HEADS * IDX_DIM


def _params(n_axes):
    return pltpu.CompilerParams(dimension_semantics=("arbitrary",) * n_axes,
                                vmem_limit_bytes=VMEM_LIMIT_BYTES)


def _const_spec(shape):
    nd = len(shape)
    return pl.BlockSpec(shape, lambda *_: (0,) * nd, pipeline_mode=pl.Buffered(1))


def _rms(x, g):
    return x * lax.rsqrt(jnp.mean(x * x, axis=-1, keepdims=True) + EPS) * g


def _dot(a, b):
    return jnp.dot(a, b, preferred_element_type=F32)


def _dot_nt(a, b):
    return lax.dot_general(a, b, (((1,), (1,)), ((), ())), preferred_element_type=F32)


def _ff_chunks(d_ff):
    step = 4 * MXU_COLS
    return [(s, min(step, d_ff - s)) for s in range(0, d_ff, step)]


def _swiglu_half(x, g_ref, wg_ref, wu_ref, wd_ref):
    h = _rms(x, g_ref[...]).astype(BF16)
    acc = None
    for start, width in _ff_chunks(wg_ref.shape[1]):
        gate = _dot(h, wg_ref[:, start:start + width])
        up = _dot(h, wu_ref[:, start:start + width])
        act = (gate * (1.0 / (1.0 + jnp.exp(-gate))) * up).astype(BF16)
        part = _dot(act, wd_ref[start:start + width, :])
        acc = part if acc is None else acc + part
    return x + 0.5 * acc


def _ffn_kernel(x_ref, g_ref, wg_ref, wu_ref, wd_ref, o_ref):
    o_ref[...] = _swiglu_half(x_ref[...], g_ref, wg_ref, wu_ref, wd_ref)


def _out_ffn_kernel(x_ref, oa_ref, ob_ref, woa_ref, wob_ref, g_ref, wg_ref, wu_ref, wd_ref, o_ref):
    x = x_ref[...] + _dot(oa_ref[...], woa_ref[...]) + _dot(ob_ref[...], wob_ref[...])
    o_ref[...] = _swiglu_half(x, g_ref, wg_ref, wu_ref, wd_ref)


def _row_tile(n):
    for tm in (512, 256, 128, 64, 32, 16, 8):
        if n % tm == 0:
            return tm
    raise ValueError(f"row count {n} is not a multiple of 8")


def _ffn(x, g, wg, wu, wd):
    n, d = x.shape
    tm = _row_tile(n)
    row = pl.BlockSpec((tm, d), lambda i: (i, 0))
    return pl.pallas_call(
        _ffn_kernel, grid=(n // tm,),
        in_specs=[row, _const_spec(g.shape), _const_spec(wg.shape), _const_spec(wu.shape), _const_spec(wd.shape)],
        out_specs=row, out_shape=jax.ShapeDtypeStruct((n, d), F32),
        compiler_params=_params(1), name="ffn")(x, g, wg, wu, wd)


def _out_ffn(x, oa, ob, woa, wob, g, wg, wu, wd):
    n, d = x.shape
    tm = _row_tile(n)
    row = pl.BlockSpec((tm, d), lambda i: (i, 0))
    half = pl.BlockSpec((tm, oa.shape[1]), lambda i: (i, 0))
    consts = [woa, wob, g, wg, wu, wd]
    return pl.pallas_call(
        _out_ffn_kernel, grid=(n // tm,),
        in_specs=[row, half, half] + [_const_spec(c.shape) for c in consts],
        out_specs=row, out_shape=jax.ShapeDtypeStruct((n, d), F32),
        compiler_params=_params(1), name="out_ffn")(x, oa, ob, *consts)


def _dup_halves(t, first):
    lane = lax.broadcasted_iota(I32, t.shape, 1)
    rolled = pltpu.roll(t, HEAD_DIM, axis=1)
    if first:
        return jnp.where(lane < HEAD_DIM, t, rolled)
    return jnp.where(lane < HEAD_DIM, rolled, t)


def _proj_kernel(x_ref, g_ref, win_ref, gdq_ref, gdk_ref, gsq_ref, gsk_ref, g512_ref, g128_ref,
                 dkc_ref, dvc_ref, skc_ref, svc_ref, ikc_ref,
                 dq_ref, dkb_ref, dvt_ref, sq_ref, skk_ref, svt_ref, iq_ref, ikk_ref, iwt_ref):
    h = _rms(x_ref[...], g_ref[...]).astype(BF16)
    z = _dot(h, win_ref[...])
    n_sub, _, t_sub = dvt_ref.shape

    def headnorm(seg, ones_ref, gain):
        ss = _dot((seg * seg).astype(BF16), ones_ref[...])
        return seg * lax.rsqrt(ss * (1.0 / HEAD_DIM) + EPS) * gain

    dq_ref[...] = headnorm(z[:, _C_DQ:_C_DQ + DIFF_QK], g512_ref, gdq_ref[...]).astype(BF16)
    dk = headnorm(z[:, _C_DK:_C_DK + DIFF_QK], g512_ref, gdk_ref[...])
    dkc_ref[...] = dk
    dkb_ref[...] = dk.astype(BF16)
    sq_ref[...] = headnorm(z[:, _C_SQ:_C_SQ + DSA_WIDTH], g512_ref, gsq_ref[...]).astype(BF16)

    dv = z[:, _C_DV:_C_DV + DIFF_WIDTH]
    dvc_ref[...] = dv
    ones_a = jnp.ones((ONES_ROWS, t_sub), BF16)
    for s in range(n_sub):
        for hh in range(DIFF_HEADS):
            blk = dv[s * t_sub:(s + 1) * t_sub, hh * DIFF_V_DIM:(hh + 1) * DIFF_V_DIM]
            dvt_ref[s, hh * VA_ROWS:hh * VA_ROWS + DIFF_V_DIM, :] = blk.T.astype(BF16)
            dvt_ref[s, hh * VA_ROWS + DIFF_V_DIM:(hh + 1) * VA_ROWS, :] = ones_a

    skv = z[:, _C_SKV:_C_SKV + LANES]
    skn = headnorm(skv, g128_ref, gsk_ref[...])
    skk = _dup_halves(skn, first=True)
    svv = _dup_halves(skv, first=False)
    skc_ref[...] = skk[:, :HEAD_DIM]
    svc_ref[...] = svv[:, :HEAD_DIM]
    skk_ref[...] = skk.astype(BF16)
    for s in range(n_sub):
        svt_ref[s, 0:HEAD_DIM, :] = svv[s * t_sub:(s + 1) * t_sub, :].T[:HEAD_DIM, :].astype(BF16)
        svt_ref[s, HEAD_DIM:VB_ROWS, :] = ones_a

    iq_ref[...] = z[:, _C_IQ:_C_IQ + IDX_HEADS * IDX_DIM].astype(BF16)
    ikw = z[:, _C_IKW:_C_IKW + LANES]
    ikk = _dup_halves(ikw, first=True)
    ikc_ref[...] = ikk[:, :HEAD_DIM]
    ikk_ref[...] = ikk.astype(BF16)
    iwt_ref[...] = ikw.T[HEAD_DIM:HEAD_DIM + 8, :] * IDX_SCALE


def _proj(x1, g_mix, w_in, g_dq, g_dk, g_sq, g_sk):
    n, d = x1.shape
    tm = _row_tile(n)
    t_sub = min(KEY_TILE, tm)
    n_sub = tm // t_sub
    nt = n // t_sub

    def tile8(g, scale):
        return (jnp.tile(g.astype(F32), DIFF_QK // HEAD_DIM) * scale)[None, :]

    gsk_pad = jnp.concatenate([g_sk.astype(F32), jnp.ones((HEAD_DIM,), F32)])[None, :]
    grp = jnp.arange(DIFF_QK) // HEAD_DIM
    ones512 = (grp[:, None] == grp[None, :]).astype(BF16)
    ones128 = ones512[:LANES, :LANES]
    consts = [g_mix, w_in, tile8(g_dq, QK_SCALE * LOG2E), tile8(g_dk, 1.0), tile8(g_sq, QK_SCALE * LOG2E), gsk_pad,
              ones512, ones128]

    def rows(width):
        return pl.BlockSpec((tm, width), lambda i: (i, 0))

    def tposed(nrows):
        return pl.BlockSpec((n_sub, nrows, t_sub), lambda i: (i, 0, 0))

    out_shapes = [
        jax.ShapeDtypeStruct((n, DIFF_QK), F32), jax.ShapeDtypeStruct((n, DIFF_WIDTH), F32),
        jax.ShapeDtypeStruct((n, HEAD_DIM), F32), jax.ShapeDtypeStruct((n, HEAD_DIM), F32),
        jax.ShapeDtypeStruct((n, IDX_DIM), F32),
        jax.ShapeDtypeStruct((n, DIFF_QK), BF16), jax.ShapeDtypeStruct((n, DIFF_QK), BF16),
        jax.ShapeDtypeStruct((nt, DIFF_HEADS * VA_ROWS, t_sub), BF16),
        jax.ShapeDtypeStruct((n, DSA_WIDTH), BF16), jax.ShapeDtypeStruct((n, LANES), BF16),
        jax.ShapeDtypeStruct((nt, VB_ROWS, t_sub), BF16),
        jax.ShapeDtypeStruct((n, IDX_HEADS * IDX_DIM), BF16), jax.ShapeDtypeStruct((n, LANES), BF16),
        jax.ShapeDtypeStruct((8, n), F32),
    ]
    out_specs = [rows(DIFF_QK), rows(DIFF_WIDTH), rows(HEAD_DIM), rows(HEAD_DIM), rows(IDX_DIM),
                 rows(DIFF_QK), rows(DIFF_QK), tposed(DIFF_HEADS * VA_ROWS),
                 rows(DSA_WIDTH), rows(LANES), tposed(VB_ROWS),
                 rows(IDX_HEADS * IDX_DIM), rows(LANES), pl.BlockSpec((8, tm), lambda i: (0, i))]
    outs = pl.pallas_call(
        _proj_kernel, grid=(n // tm,),
        in_specs=[rows(d)] + [_const_spec(c.shape) for c in consts],
        out_specs=out_specs, out_shape=out_shapes,
        compiler_params=_params(1), name="proj")(x1, *consts)
    names = ("dk_c", "dv_c", "sk_c", "sv_c", "ik_c", "dq", "dk", "dvt", "sq", "skk", "svt", "iq", "ikk", "iwt")
    return dict(zip(names, outs))


def _half_masked_weights(q_tile, n_pairs):
    lane = lax.broadcasted_iota(I32, (q_tile.shape[0], LANES), 1)
    parts = []
    for pr in range(n_pairs):
        qp = q_tile[:, pr * LANES:(pr + 1) * LANES].astype(F32)
        parts.append(jnp.where(lane < HEAD_DIM, qp, 0.0).astype(q_tile.dtype))
        parts.append(jnp.where(lane >= HEAD_DIM, qp, 0.0).astype(q_tile.dtype))
    return jnp.concatenate(parts, axis=0)


def _softmax_step(s, m_old):
    m_new = jnp.maximum(m_old, jnp.max(s, axis=0, keepdims=True).astype(F32))
    alpha = jnp.exp2(m_old - m_new)
    p = jnp.exp2(s - m_new.astype(s.dtype))
    return m_new, alpha, p


def _run_tiles(i, scores, attend):
    def run(parts):
        pending = [scores(j0, width) for j0, width, _ in parts]
        for (j0, width, near), s in zip(parts, pending):
            attend(j0, width, near, s)

    def far_quad(p, carry):
        run([(4 * p, 2, False), (4 * p + 2, 2, False)])
        return carry

    def tail(j, rem):
        parts = []
        if rem >= 2:
            parts.append((j, 2, False))
        if rem % 2:
            parts.append((j + rem - 1, 1, False))
        return parts + [(j + rem, 2, True)]

    if isinstance(i, int):
        n_far = max(i - 1, 0)
        lax.fori_loop(0, n_far // 4, far_quad, 0)
        run(tail(n_far - n_far % 4, n_far % 4) if i >= 1 else [(0, 1, True)])
    else:
        n_far = jnp.maximum(i - 1, 0)
        n_quad = lax.shift_right_logical(n_far, 2)
        lax.fori_loop(0, n_quad, far_quad, 0)
        for rem in range(4):
            pl.when((i >= 1) & ((n_far & 3) == rem))(functools.partial(run, tail(4 * n_quad, rem)))
        pl.when(i == 0)(functools.partial(run, [(0, 1, True)]))


def _row_start(j, tk):
    return j * tk if isinstance(j, int) else pl.multiple_of(j * tk, tk)


def _attn_a_kernel(lamv_ref, q_ref, k_ref, vt_ref, bias_ref, gsub_ref, o_ref, m_ref, acc_ref,
                   *, q_tile_index, lam_init):
    tq = q_ref.shape[0]
    tk = vt_ref.shape[2]
    i = pl.program_id(1) if q_tile_index is None else q_tile_index
    weights = _half_masked_weights(q_ref[...], DIFF_HEADS)
    m_ref[...] = jnp.full(m_ref.shape, M_INIT, F32)
    acc_ref[...] = jnp.zeros(acc_ref.shape, F32)

    def scores(j0, width):
        keys = pl.ds(_row_start(j0, tk), width * tk)
        return [_dot_nt(k_ref[keys, h * LANES:(h + 1) * LANES], weights[2 * h * tq:(2 * h + 2) * tq, :])
                for h in range(DIFF_HEADS)]

    def attend(j0, width, near, scored):
        for h in range(DIFF_HEADS):
            s = scored[h].astype(BF16)
            if near:
                b = bias_ref[h] if width == 2 else bias_ref[h, tk:, :]
                s = s + jnp.concatenate([b, b], axis=1)
            m_new, alpha, p = _softmax_step(s, m_ref[h])
            m_ref[h] = m_new
            vt = jnp.concatenate([vt_ref[j0 + w, h * VA_ROWS:(h + 1) * VA_ROWS, :] for w in range(width)], axis=1)
            acc_ref[h] = alpha * acc_ref[h] + _dot(vt, p)

    _run_tiles(i, scores, attend)

    lv = lamv_ref[...]
    lam = (jnp.exp(jnp.sum(lv[0:1] * lv[1:2], axis=1, keepdims=True))
           - jnp.exp(jnp.sum(lv[2:3] * lv[3:4], axis=1, keepdims=True)) + lam_init)
    for h in range(DIFF_HEADS):
        acc = acc_ref[h]
        o = acc[:DIFF_V_DIM, :] / acc[DIFF_V_DIM:DIFF_V_DIM + 1, :]
        d = o[:, :tq] - lam * o[:, tq:]
        y = d * lax.rsqrt(jnp.mean(d * d, axis=0, keepdims=True) + EPS) * gsub_ref[...] * (1.0 - lam_init)
        o_ref[:, h * DIFF_V_DIM:(h + 1) * DIFF_V_DIM] = y.T.astype(BF16)


def _attn_a(lamv, q, k, vt, bias, gsub, *, batch, nq, tq, tkeys, q_tile_index, lam_init):
    tk = vt.shape[2]
    nkt = tkeys // tk
    kern = functools.partial(_attn_a_kernel, q_tile_index=q_tile_index, lam_init=lam_init)
    return pl.pallas_call(
        kern, grid=(batch, nq),
        in_specs=[_const_spec(lamv.shape),
                  pl.BlockSpec((tq, DIFF_QK), lambda b, i: (b * nq + i, 0)),
                  pl.BlockSpec((tkeys, DIFF_QK), lambda b, i: (b, 0)),
                  pl.BlockSpec((nkt, DIFF_HEADS * VA_ROWS, tk), lambda b, i: (b, 0, 0)),
                  _const_spec(bias.shape), _const_spec(gsub.shape)],
        out_specs=pl.BlockSpec((tq, DIFF_WIDTH), lambda b, i: (b * nq + i, 0)),
        out_shape=jax.ShapeDtypeStruct((batch * nq * tq, DIFF_WIDTH), BF16),
        scratch_shapes=[pltpu.VMEM((DIFF_HEADS, 1, 2 * tq), F32),
                        pltpu.VMEM((DIFF_HEADS, VA_ROWS, 2 * tq), F32)],
        compiler_params=_params(2), name="attn_a")(lamv, q, k, vt, bias, gsub)


def _key_to_f32(key):
    return pltpu.bitcast(key ^ ((key >> 31) & 0x7FFFFFFF), F32)


def _attn_b_kernel(iq_ref, ikk_ref, iwt_ref, sq_ref, skk_ref, svt_ref, bias_ref, vis_ref, tri_ref, o_ref,
                   sc_ref, hi_ref, lo_ref, m_ref, acc_ref, tie_ref, *, q_tile_index, n_sel):
    tq = iq_ref.shape[0]
    tk = svt_ref.shape[2]
    static = q_tile_index is not None
    i = q_tile_index if static else pl.program_id(1)
    n_tiles = i + 1
    n_pairs = (n_tiles + 1) // 2 if static else lax.shift_right_logical(n_tiles + 1, 1)
    neg_inf = float("-inf")

    w_idx = _half_masked_weights(iq_ref[...], IDX_HEADS // 2)
    iw = iwt_ref[...]

    def index_scores(j, visible):
        row0 = _row_start(j, tk)
        rel = _dot_nt(ikk_ref[pl.ds(row0, tk), :], w_idx)
        score = None
        for h in range(IDX_HEADS):
            term = iw[h:h + 1, :] * jnp.maximum(rel[:, h * tq:(h + 1) * tq], 0.0)
            score = term if score is None else score + term
        if visible is not None:
            score = jnp.where(visible > 0.0, score, neg_inf)
        sc_ref[pl.ds(row0, tk), :] = score
        top = pltpu.bitcast(pltpu.bitcast(score, I32) & jnp.int32(-65536), F32)
        hi_ref[pl.ds(row0, tk), :] = top.astype(jnp.bfloat16)

    def idx_body(j, carry):
        index_scores(j, None)
        return carry

    lax.fori_loop(0, i, idx_body, 0)
    index_scores(i, vis_ref[...])
    pad0 = _row_start(n_tiles, tk)
    sc_ref[pl.ds(pad0, tk), :] = jnp.full((tk, tq), neg_inf, F32)
    hi_ref[pl.ds(pad0, tk), :] = jnp.full((tk, tq), neg_inf, jnp.bfloat16)

    n_slab = 2 * tk // COUNT_ROWS

    def count(pred):
        def body(pp, acc):
            row0 = pl.multiple_of(pp * (2 * tk), 2 * tk)
            hit = pred(sc_ref[pl.ds(row0, 2 * tk), :], row0).astype(F32)
            return acc + jnp.sum(hit.reshape(n_slab, COUNT_ROWS, tq), axis=0)
        acc = lax.fori_loop(0, n_pairs, body, jnp.zeros((COUNT_ROWS, tq), F32))
        return jnp.sum(acc, axis=0, keepdims=True)

    def count_packed(ref, cand):
        one, zero = jnp.ones((), ref.dtype), jnp.zeros((), ref.dtype)

        def body(pp, acc):
            row0 = pl.multiple_of(pp * (2 * tk), 2 * tk)
            hit = jnp.where(ref[pl.ds(row0, 2 * tk), :] >= cand, one, zero)
            slabs = [hit[r * COUNT_ROWS:(r + 1) * COUNT_ROWS] for r in range(n_slab)]
            while len(slabs) > 1:
                slabs = [a + b for a, b in zip(slabs[0::2], slabs[1::2])]
            return acc + slabs[0]
        acc = lax.fori_loop(0, n_pairs, body, jnp.zeros((COUNT_ROWS, tq), ref.dtype))
        return jnp.sum(acc.astype(F32), axis=0, keepdims=True)

    def top_bit(p, t):
        cand = t + jnp.left_shift(jnp.int32(1), 31 - p)
        cand_bf = pltpu.bitcast(cand ^ ((cand >> 31) & 0x7FFF0000), F32).astype(jnp.bfloat16)
        ok = (count_packed(hi_ref, cand_bf) >= n_sel) | (cand < KEY_NEG_INF)
        return jnp.where(ok, cand, t)

    base = lax.fori_loop(0, 16, top_bit, jnp.full((1, tq), INT_MIN, I32))
    zero_bucket = base == 0

    def settle_zero():
        cnt = count(lambda tile, _: tile >= 0.0)
        return jnp.where(zero_bucket & (cnt < n_sel), jnp.int32(-65536), base)

    base = lax.cond(jnp.max(zero_bucket.astype(I32)) > 0, settle_zero, lambda: base)

    base_hi = base >> 16

    def fill_low(pp, carry):
        row0 = pl.multiple_of(pp * (2 * tk), 2 * tk)
        bits = pltpu.bitcast(sc_ref[pl.ds(row0, 2 * tk), :], I32)
        key = bits ^ ((bits >> 31) & 0x7FFFFFFF)
        bucket = key >> 16
        low = jnp.where(bucket > base_hi, 32767, jnp.where(bucket < base_hi, -32768, (key & 0xFFFF) - 32768))
        lo_ref[pl.ds(row0, 2 * tk), :] = low.astype(jnp.int16)
        return carry

    lax.fori_loop(0, n_pairs, fill_low, 0)

    def low_bit(p, t):
        cand = t + jnp.left_shift(jnp.int32(1), 15 - p)
        cnt = count_packed(lo_ref, (cand - 32768).astype(jnp.int16))
        ok = (cnt >= n_sel) | (base + cand < KEY_NEG_INF)
        return jnp.where(ok, cand, t)

    thr_key = base + lax.fori_loop(0, 16, low_bit, jnp.zeros((1, tq), I32))
    thr = jnp.maximum(_key_to_f32(thr_key), jnp.finfo(F32).min)
    tie_budget = n_sel - count(lambda tile, _: tile > thr)

    w_q = _half_masked_weights(sq_ref[...], DSA_HEADS // 2)
    m_ref[...] = jnp.full(m_ref.shape, M_INIT, F32)
    acc_ref[...] = jnp.zeros(acc_ref.shape, F32)
    tie_ref[...] = jnp.zeros(tie_ref.shape, F32)

    def scores(j0, width):
        row0 = _row_start(j0, tk)
        rows = width * tk
        sc = sc_ref[pl.ds(row0, rows), :]
        tie = sc == thr
        tie_one = jnp.where(tie, 1.0, 0.0).astype(jnp.bfloat16)
        seen = tie_ref[...]
        ranks = []
        for w in range(width):
            ranks.append(_dot(tri_ref[...], tie_one[w * tk:(w + 1) * tk]) + seen)
            seen = ranks[-1][tk - 1:tk, :]
        tie_ref[...] = seen
        sel = (sc > thr) | (tie & (jnp.concatenate(ranks, axis=0) <= tie_budget))
        masked = jnp.where(sel, 0.0, S_MASKED).astype(BF16)
        return masked, _dot_nt(skk_ref[pl.ds(row0, rows), :], w_q)

    def attend(j0, width, near, scored):
        masked, s_all = scored
        parts = []
        for h in range(DSA_HEADS):
            s = s_all[:, h * tq:(h + 1) * tq].astype(BF16) + masked
            if near:
                s = s + (bias_ref[h] if width == 2 else bias_ref[h, tk:, :])
            parts.append(s)
        m_new, alpha, p = _softmax_step(jnp.concatenate(parts, axis=1), m_ref[...])
        vt = jnp.concatenate([svt_ref[j0 + w] for w in range(width)], axis=1)
        acc_ref[...] = alpha * acc_ref[...] + _dot(vt, p)
        m_ref[...] = m_new

    _run_tiles(i, scores, attend)

    acc = acc_ref[...]
    o = acc[:HEAD_DIM, :] / acc[HEAD_DIM:HEAD_DIM + 1, :]
    for pr in range(DSA_HEADS // 2):
        pair = jnp.concatenate([o[:, 2 * pr * tq:(2 * pr + 1) * tq], o[:, (2 * pr + 1) * tq:(2 * pr + 2) * tq]],
                               axis=0)
        o_ref[:, pr * LANES:(pr + 1) * LANES] = pair.T.astype(BF16)


def _attn_b(p, bias, vis, *, batch, nq, tq, tkeys, q_tile_index, n_sel):
    tk = p["svt"].shape[2]
    nkt = tkeys // tk
    assert (tkeys + tk) // COUNT_ROWS <= 256, "bf16 partial counts must stay exact"
    kern = functools.partial(_attn_b_kernel, q_tile_index=q_tile_index, n_sel=n_sel)
    tri = jnp.tril(jnp.ones((tk, tk), jnp.bfloat16))
    qrow = lambda b, i: (b * nq + i, 0)
    krow = lambda b, i: (b, 0)
    return pl.pallas_call(
        kern, grid=(batch, nq),
        in_specs=[pl.BlockSpec((tq, IDX_HEADS * IDX_DIM), qrow),
                  pl.BlockSpec((tkeys, LANES), krow),
                  pl.BlockSpec((8, tq), lambda b, i: (0, b * nq + i)),
                  pl.BlockSpec((tq, DSA_WIDTH), qrow),
                  pl.BlockSpec((tkeys, LANES), krow),
                  pl.BlockSpec((nkt, VB_ROWS, tk), lambda b, i: (b, 0, 0)),
                  _const_spec(bias.shape), _const_spec(vis.shape), _const_spec(tri.shape)],
        out_specs=pl.BlockSpec((tq, DSA_WIDTH), qrow),
        out_shape=jax.ShapeDtypeStruct((batch * nq * tq, DSA_WIDTH), BF16),
        scratch_shapes=[pltpu.VMEM((tkeys + tk, tq), F32),
                        pltpu.VMEM((tkeys + tk, tq), jnp.bfloat16),
                        pltpu.VMEM((tkeys + tk, tq), jnp.int16),
                        pltpu.VMEM((1, DSA_HEADS * tq), F32),
                        pltpu.VMEM((VB_ROWS, DSA_HEADS * tq), F32),
                        pltpu.VMEM((1, tq), F32)],
        compiler_params=_params(2), name="attn_b")(
            p["iq"], p["ikk"], p["iwt"], p["sq"], p["skk"], p["svt"], bias, vis, tri)


def _t5_bucket(rel):
    half = NUM_BUCKETS // 2
    exact = half // 2
    n = jnp.abs(rel)
    logn = jnp.log(jnp.maximum(n, 1).astype(F32) / exact)
    large = exact + (logn / math.log(MAX_DISTANCE / exact) * (half - exact)).astype(I32)
    large = jnp.minimum(large, half - 1)
    return jnp.where(rel > 0, half, 0) + jnp.where(n < exact, n, large)


def _bias_lookup(rel_bias, bucket):
    out = jnp.zeros((rel_bias.shape[1],) + bucket.shape, F32)
    for b in range(NUM_BUCKETS):
        out = out + jnp.where(bucket[None] == b, rel_bias[b].astype(F32)[:, None, None], 0.0)
    return out


def _bias_tiles(rel_bias, tk, tq, q_valid, k_valid_diag):
    kk = jnp.arange(tk, dtype=I32)[:, None]
    qq = jnp.arange(tq, dtype=I32)[None, :]
    far = _bias_lookup(rel_bias, _t5_bucket(jnp.full((1, 1), -(2 * MAX_DISTANCE), I32)))
    tiles = [(_bias_lookup(rel_bias, _t5_bucket(kk + d - qq)) - far) * LOG2E for d in (-tk, 0)]
    vis = ((kk // CHUNK) <= (qq // CHUNK)) & (kk < k_valid_diag) | (qq >= q_valid)
    tiles[1] = jnp.where(vis[None], tiles[1], M_INIT)
    return jnp.concatenate(tiles, axis=1).astype(BF16), vis.astype(F32)


def _token_mix(p, rel_bias, lamv, gsub, *, batch, nq, tq, tkeys, q_tile_index, q_valid, k_valid_diag,
               n_sel, lam_init):
    bias, vis = _bias_tiles(rel_bias, KEY_TILE, tq, q_valid, k_valid_diag)
    gsub_b = jnp.broadcast_to(gsub.astype(F32)[:, None], (DIFF_V_DIM, tq))
    o_a = _attn_a(lamv, p["dq"], p["dk"], p["dvt"], bias[:DIFF_HEADS], gsub_b, batch=batch, nq=nq, tq=tq,
                  tkeys=tkeys, q_tile_index=q_tile_index, lam_init=lam_init)
    o_b = _attn_b(p, bias[DIFF_HEADS:], vis, batch=batch, nq=nq, tq=tq, tkeys=tkeys,
                  q_tile_index=q_tile_index, n_sel=n_sel)
    return o_a, o_b


def _pad_rows(a, rows):
    return jnp.pad(a, ((0, 0), (0, rows - a.shape[1])) + ((0, 0),) * (a.ndim - 2))


def kernel(x_prompt, x_sample, cache_diff_k, cache_diff_v, cache_dsa_k, cache_dsa_v, cache_idx_k, g_ffn1, w1_gate, w1_up, w1_down, g_mix, w_in, g_dq, g_dk, g_sq, g_sk, lam_q1, lam_k1, lam_q2, lam_k2, g_subln, w_out, g_ffn2, w2_gate, w2_up, w2_down, rel_bias):
    depth = g_ffn1.shape[0]
    assert depth == 1, "one layer"
    l = 0
    batch, seq, d_model = x_prompt.shape
    sbatch, sseq, _ = x_sample.shape
    past = cache_diff_k.shape[2]
    tk = KEY_TILE
    assert seq % Q_TILE == 0 and Q_TILE % CHUNK == 0 and Q_TILE == tk
    assert past % tk == 0 and past % CHUNK == 0 and sseq <= CHUNK and sseq % 8 == 0
    lam_init = 0.8 - 0.6 * math.exp(-0.3 * l)

    bf = lambda w: w.astype(BF16)
    row = lambda g: g.astype(F32)[None, :]
    w1 = (row(g_ffn1[l]), bf(w1_gate[l]), bf(w1_up[l]), bf(w1_down[l]))
    w2 = (row(g_ffn2[l]), bf(w2_gate[l]), bf(w2_up[l]), bf(w2_down[l]))
    w_in_p = jnp.pad(bf(w_in[l]), ((0, 0), (0, IN_COLS_PAD - IN_COLS)))
    wo_a, wo_b = bf(w_out[l][:DIFF_WIDTH]), bf(w_out[l][DIFF_WIDTH:])
    lamv = jnp.pad(jnp.stack([lam_q1[l], lam_k1[l], lam_q2[l], lam_k2[l]]).astype(F32), ((0, 4), (0, LANES - HEAD_DIM)))
    proj = functools.partial(_proj, g_mix=row(g_mix[l]), w_in=w_in_p, g_dq=g_dq[l], g_dk=g_dk[l], g_sq=g_sq[l],
                             g_sk=g_sk[l])

    xp1 = _ffn(x_prompt.reshape(batch * seq, d_model), *w1)
    pp = proj(xp1)
    oa_p, ob_p = _token_mix(pp, rel_bias, lamv, g_subln[l], batch=batch, nq=seq // Q_TILE, tq=Q_TILE, tkeys=seq,
                            q_tile_index=None, q_valid=Q_TILE, k_valid_diag=tk,
                            n_sel=min(TOPK_MAX, seq // 4), lam_init=lam_init)
    yp = _out_ffn(xp1, oa_p, ob_p, wo_a, wo_b, *w2).reshape(batch, seq, d_model)

    ns = sbatch * sseq
    xs1 = _ffn(x_sample.reshape(ns, d_model), *w1)
    ps = proj(xs1)
    tq_s = LANES
    keys_s = past + tk
    per_b = lambda a: a.reshape(sbatch, sseq, a.shape[-1])

    def q_rows(a):
        return _pad_rows(per_b(a), tq_s).reshape(sbatch * tq_s, a.shape[-1])

    def k_rows(cache, new):
        both = jnp.concatenate([cache.astype(BF16), per_b(new)], axis=1)
        return _pad_rows(both, keys_s).reshape(sbatch * keys_s, new.shape[-1])

    def v_tiles(cache, new_t, rows, heads):
        rv = rows - ONES_ROWS
        c = cache.astype(BF16).reshape(sbatch, past // tk, tk, heads, rv)
        c = jnp.transpose(c, (0, 1, 3, 4, 2))
        c = jnp.concatenate([c, jnp.ones((sbatch, past // tk, heads, ONES_ROWS, tk), BF16)], axis=3)
        c = c.reshape(sbatch, past // tk, heads * rows, tk)
        n = new_t.reshape(heads * rows, sbatch, sseq)
        n = jnp.pad(jnp.transpose(n, (1, 0, 2)), ((0, 0), (0, 0), (0, tk - sseq)))[:, None]
        return jnp.concatenate([c, n], axis=1).reshape(sbatch * (past // tk + 1), heads * rows, tk)

    dup = lambda c: jnp.concatenate([c, c], axis=-1)
    iwt_s = jnp.pad(ps["iwt"].reshape(8, sbatch, sseq), ((0, 0), (0, 0), (0, tq_s - sseq))).reshape(8, sbatch * tq_s)
    psa = {
        "dq": q_rows(ps["dq"]), "sq": q_rows(ps["sq"]), "iq": q_rows(ps["iq"]), "iwt": iwt_s,
        "dk": k_rows(cache_diff_k[l].reshape(sbatch, past, DIFF_QK), ps["dk"]),
        "skk": k_rows(dup(cache_dsa_k[l]), ps["skk"]),
        "ikk": k_rows(dup(cache_idx_k[l]), ps["ikk"]),
        "dvt": v_tiles(cache_diff_v[l].reshape(sbatch, past, DIFF_WIDTH), ps["dvt"], VA_ROWS, DIFF_HEADS),
        "svt": v_tiles(cache_dsa_v[l], ps["svt"], VB_ROWS, 1),
    }
    oa_s, ob_s = _token_mix(psa, rel_bias, lamv, g_subln[l], batch=sbatch, nq=1, tq=tq_s, tkeys=keys_s,
                            q_tile_index=past // tk, q_valid=sseq, k_valid_diag=sseq,
                            n_sel=min(TOPK_MAX, (past + sseq) // 4), lam_init=lam_init)
    take = lambda o: o.reshape(sbatch, tq_s, o.shape[-1])[:, :sseq].reshape(ns, o.shape[-1])
    ys = _out_ffn(xs1, take(oa_s), take(ob_s), wo_a, wo_b, *w2).reshape(sbatch, sseq, d_model)

    def caches(p, b, t):
        return (p["dk_c"].reshape(depth, b, t, DIFF_HEADS, 2, HEAD_DIM),
                p["dv_c"].reshape(depth, b, t, DIFF_HEADS, DIFF_V_DIM),
                p["sk_c"].reshape(depth, b, t, HEAD_DIM), p["sv_c"].reshape(depth, b, t, HEAD_DIM),
                p["ik_c"].reshape(depth, b, t, IDX_DIM))

    return (yp, ys) + caches(pp, batch, seq) + caches(ps, sbatch, sseq)
```

```python
import functools
import math

import jax
import jax.numpy as jnp
from jax import lax
from jax.experimental import pallas as pl
from jax.experimental.pallas import tpu as pltpu

F32 = jnp.float32
BF16 = jnp.bfloat16
I32 = jnp.int32

CHUNK = 64
HEAD_DIM = 64
DIFF_HEADS = 4
DIFF_V_DIM = 2 * HEAD_DIM
DSA_HEADS = 8
IDX_HEADS = 4
IDX_DIM = 64
TOPK_MAX = 256
NUM_BUCKETS = 32
MAX_DISTANCE = 128
EPS = 1e-6
IDX_SCALE = (IDX_HEADS ** -0.5) * (IDX_DIM ** -0.5)
QK_SCALE = HEAD_DIM ** -0.5
LOG2E = math.log2(math.e)

DIFF_QK = DIFF_HEADS * 2 * HEAD_DIM
DIFF_WIDTH = DIFF_HEADS * DIFF_V_DIM
DSA_WIDTH = DSA_HEADS * HEAD_DIM
IN_COLS = 3 * DIFF_QK + DSA_WIDTH + 2 * HEAD_DIM + IDX_HEADS * IDX_DIM + IDX_DIM + IDX_HEADS

LANES = 128
BF16_SUBLANES = 16
MXU_COLS = 256
VMEM_LIMIT_BYTES = 60000 * 1024

IN_COLS_PAD = -(-IN_COLS // MXU_COLS) * MXU_COLS
ONES_ROWS = BF16_SUBLANES
VA_ROWS = DIFF_V_DIM + ONES_ROWS
VB_ROWS = HEAD_DIM + ONES_ROWS

KEY_TILE = 256
Q_TILE = 256
M_INIT = -(2.0 ** 100)
S_MASKED = -2e30
INT_MIN = -(2 ** 31)
KEY_NEG_INF = INT_MIN + 0x7FFFFF
COUNT_ROWS = 32

_C_DQ, _C_DK, _C_DV, _C_SQ = 0, DIFF_QK, 2 * DIFF_QK, 3 * DIFF_QK
_C_SKV = _C_SQ + DSA_WIDTH
_C_IQ = _C_SKV + 2 * HEAD_DIM
_C_IKW = _C_IQ + IDX_HEADS * IDX_DIM


def _params(n_axes):
    return pltpu.CompilerParams(dimension_semantics=("arbitrary",) * n_axes,
                                vmem_limit_bytes=VMEM_LIMIT_BYTES)


def _const_spec(shape):
    nd = len(shape)
    return pl.BlockSpec(shape, lambda *_: (0,) * nd, pipeline_mode=pl.Buffered(1))


def _rms(x, g):
    return x * lax.rsqrt(jnp.mean(x * x, axis=-1, keepdims=True) + EPS) * g


def _dot(a, b):
    return jnp.dot(a, b, preferred_element_type=F32)


def _dot_nt(a, b):
    return lax.dot_general(a, b, (((1,), (1,)), ((), ())), preferred_element_type=F32)


def _ff_chunks(d_ff):
    step = 4 * MXU_COLS
    return [(s, min(step, d_ff - s)) for s in range(0, d_ff, step)]


def _swiglu_half(x, g_ref, wg_ref, wu_ref, wd_ref):
    h = _rms(x, g_ref[...]).astype(BF16)
    acc = None
    for start, width in _ff_chunks(wg_ref.shape[1]):
        gate = _dot(h, wg_ref[:, start:start + width])
        up = _dot(h, wu_ref[:, start:start + width])
        act = (gate * (1.0 / (1.0 + jnp.exp(-gate))) * up).astype(BF16)
        part = _dot(act, wd_ref[start:start + width, :])
        acc = part if acc is None else acc + part
    return x + 0.5 * acc


def _ffn_kernel(x_ref, g_ref, wg_ref, wu_ref, wd_ref, o_ref):
    o_ref[...] = _swiglu_half(x_ref[...], g_ref, wg_ref, wu_ref, wd_ref)


def _out_ffn_kernel(x_ref, oa_ref, ob_ref, woa_ref, wob_ref, g_ref, wg_ref, wu_ref, wd_ref, o_ref):
    x = x_ref[...] + _dot(oa_ref[...], woa_ref[...]) + _dot(ob_ref[...], wob_ref[...])
    o_ref[...] = _swiglu_half(x, g_ref, wg_ref, wu_ref, wd_ref)


def _row_tile(n):
    for tm in (512, 256, 128, 64, 32, 16, 8):
        if n % tm == 0:
            return tm
    raise ValueError(f"row count {n} is not a multiple of 8")


def _ffn(x, g, wg, wu, wd):
    n, d = x.shape
    tm = _row_tile(n)
    row = pl.BlockSpec((tm, d), lambda i: (i, 0))
    return pl.pallas_call(
        _ffn_kernel, grid=(n // tm,),
        in_specs=[row, _const_spec(g.shape), _const_spec(wg.shape), _const_spec(wu.shape), _const_spec(wd.shape)],
        out_specs=row, out_shape=jax.ShapeDtypeStruct((n, d), F32),
        compiler_params=_params(1), name="ffn")(x, g, wg, wu, wd)


def _out_ffn(x, oa, ob, woa, wob, g, wg, wu, wd):
    n, d = x.shape
    tm = _row_tile(n)
    row = pl.BlockSpec((tm, d), lambda i: (i, 0))
    half = pl.BlockSpec((tm, oa.shape[1]), lambda i: (i, 0))
    consts = [woa, wob, g, wg, wu, wd]
    return pl.pallas_call(
        _out_ffn_kernel, grid=(n // tm,),
        in_specs=[row, half, half] + [_const_spec(c.shape) for c in consts],
        out_specs=row, out_shape=jax.ShapeDtypeStruct((n, d), F32),
        compiler_params=_params(1), name="out_ffn")(x, oa, ob, *consts)


def _dup_halves(t, first):
    lane = lax.broadcasted_iota(I32, t.shape, 1)
    rolled = pltpu.roll(t, HEAD_DIM, axis=1)
    if first:
        return jnp.where(lane < HEAD_DIM, t, rolled)
    return jnp.where(lane < HEAD_DIM, rolled, t)


def _proj_kernel(x_ref, g_ref, win_ref, gdq_ref, gdk_ref, gsq_ref, gsk_ref, g512_ref, g128_ref,
                 dkc_ref, dvc_ref, skc_ref, svc_ref, ikc_ref,
                 dq_ref, dkb_ref, dvt_ref, sq_ref, skk_ref, svt_ref, iq_ref, ikk_ref, iwt_ref):
    h = _rms(x_ref[...], g_ref[...]).astype(BF16)
    z = _dot(h, win_ref[...])
    n_sub, _, t_sub = dvt_ref.shape

    def headnorm(seg, ones_ref, gain):
        ss = _dot((seg * seg).astype(BF16), ones_ref[...])
        return seg * lax.rsqrt(ss * (1.0 / HEAD_DIM) + EPS) * gain

    dq_ref[...] = headnorm(z[:, _C_DQ:_C_DQ + DIFF_QK], g512_ref, gdq_ref[...]).astype(BF16)
    dk = headnorm(z[:, _C_DK:_C_DK + DIFF_QK], g512_ref, gdk_ref[...])
    dkc_ref[...] = dk
    dkb_ref[...] = dk.astype(BF16)
    sq_ref[...] = headnorm(z[:, _C_SQ:_C_SQ + DSA_WIDTH], g512_ref, gsq_ref[...]).astype(BF16)

    dv = z[:, _C_DV:_C_DV + DIFF_WIDTH]
    dvc_ref[...] = dv
    ones_a = jnp.ones((ONES_ROWS, t_sub), BF16)
    for s in range(n_sub):
        for hh in range(DIFF_HEADS):
            blk = dv[s * t_sub:(s + 1) * t_sub, hh * DIFF_V_DIM:(hh + 1) * DIFF_V_DIM]
            dvt_ref[s, hh * VA_ROWS:hh * VA_ROWS + DIFF_V_DIM, :] = blk.T.astype(BF16)
            dvt_ref[s, hh * VA_ROWS + DIFF_V_DIM:(hh + 1) * VA_ROWS, :] = ones_a

    skv = z[:, _C_SKV:_C_SKV + LANES]
    skn = headnorm(skv, g128_ref, gsk_ref[...])
    skk = _dup_halves(skn, first=True)
    svv = _dup_halves(skv, first=False)
    skc_ref[...] = skk[:, :HEAD_DIM]
    svc_ref[...] = svv[:, :HEAD_DIM]
    skk_ref[...] = skk.astype(BF16)
    for s in range(n_sub):
        svt_ref[s, 0:HEAD_DIM, :] = svv[s * t_sub:(s + 1) * t_sub, :].T[:HEAD_DIM, :].astype(BF16)
        svt_ref[s, HEAD_DIM:VB_ROWS, :] = ones_a

    iq_ref[...] = z[:, _C_IQ:_C_IQ + IDX_HEADS * IDX_DIM].astype(BF16)
    ikw = z[:, _C_IKW:_C_IKW + LANES]
    ikk = _dup_halves(ikw, first=True)
    ikc_ref[...] = ikk[:, :HEAD_DIM]
    ikk_ref[...] = ikk.astype(BF16)
    iwt_ref[...] = ikw.T[HEAD_DIM:HEAD_DIM + 8, :] * IDX_SCALE


def _proj(x1, g_mix, w_in, g_dq, g_dk, g_sq, g_sk):
    n, d = x1.shape
    tm = _row_tile(n)
    t_sub = min(KEY_TILE, tm)
    n_sub = tm // t_sub
    nt = n // t_sub

    def tile8(g, scale):
        return (jnp.tile(g.astype(F32), DIFF_QK // HEAD_DIM) * scale)[None, :]

    gsk_pad = jnp.concatenate([g_sk.astype(F32), jnp.ones((HEAD_DIM,), F32)])[None, :]
    grp = jnp.arange(DIFF_QK) // HEAD_DIM
    ones512 = (grp[:, None] == grp[None, :]).astype(BF16)
    ones128 = ones512[:LANES, :LANES]
    consts = [g_mix, w_in, tile8(g_dq, QK_SCALE * LOG2E), tile8(g_dk, 1.0), tile8(g_sq, QK_SCALE * LOG2E), gsk_pad,
              ones512, ones128]

    def rows(width):
        return pl.BlockSpec((tm, width), lambda i: (i, 0))

    def tposed(nrows):
        return pl.BlockSpec((n_sub, nrows, t_sub), lambda i: (i, 0, 0))

    out_shapes = [
        jax.ShapeDtypeStruct((n, DIFF_QK), F32), jax.ShapeDtypeStruct((n, DIFF_WIDTH), F32),
        jax.ShapeDtypeStruct((n, HEAD_DIM), F32), jax.ShapeDtypeStruct((n, HEAD_DIM), F32),
        jax.ShapeDtypeStruct((n, IDX_DIM), F32),
        jax.ShapeDtypeStruct((n, DIFF_QK), BF16), jax.ShapeDtypeStruct((n, DIFF_QK), BF16),
        jax.ShapeDtypeStruct((nt, DIFF_HEADS * VA_ROWS, t_sub), BF16),
        jax.ShapeDtypeStruct((n, DSA_WIDTH), BF16), jax.ShapeDtypeStruct((n, LANES), BF16),
        jax.ShapeDtypeStruct((nt, VB_ROWS, t_sub), BF16),
        jax.ShapeDtypeStruct((n, IDX_HEADS * IDX_DIM), BF16), jax.ShapeDtypeStruct((n, LANES), BF16),
        jax.ShapeDtypeStruct((8, n), F32),
    ]
    out_specs = [rows(DIFF_QK), rows(DIFF_WIDTH), rows(HEAD_DIM), rows(HEAD_DIM), rows(IDX_DIM),
                 rows(DIFF_QK), rows(DIFF_QK), tposed(DIFF_HEADS * VA_ROWS),
                 rows(DSA_WIDTH), rows(LANES), tposed(VB_ROWS),
                 rows(IDX_HEADS * IDX_DIM), rows(LANES), pl.BlockSpec((8, tm), lambda i: (0, i))]
    outs = pl.pallas_call(
        _proj_kernel, grid=(n // tm,),
        in_specs=[rows(d)] + [_const_spec(c.shape) for c in consts],
        out_specs=out_specs, out_shape=out_shapes,
        compiler_params=_params(1), name="proj")(x1, *consts)
    names = ("dk_c", "dv_c", "sk_c", "sv_c", "ik_c", "dq", "dk", "dvt", "sq", "skk", "svt", "iq", "ikk", "iwt")
    return dict(zip(names, outs))


def _half_masked_weights(q_tile, n_pairs):
    lane = lax.broadcasted_iota(I32, (q_tile.shape[0], LANES), 1)
    parts = []
    for pr in range(n_pairs):
        qp = q_tile[:, pr * LANES:(pr + 1) * LANES].astype(F32)
        parts.append(jnp.where(lane < HEAD_DIM, qp, 0.0).astype(q_tile.dtype))
        parts.append(jnp.where(lane >= HEAD_DIM, qp, 0.0).astype(q_tile.dtype))
    return jnp.concatenate(parts, axis=0)


def _softmax_step(s, m_old):
    m_new = jnp.maximum(m_old, jnp.max(s, axis=0, keepdims=True).astype(F32))
    alpha = jnp.exp2(m_old - m_new)
    p = jnp.exp2(s - m_new.astype(s.dtype))
    return m_new, alpha, p


def _run_tiles(i, scores, attend):
    def run(parts):
        pending = [scores(j0, width) for j0, width, _ in parts]
        for (j0, width, near), s in zip(parts, pending):
            attend(j0, width, near, s)

    def far_quad(p, carry):
        run([(4 * p, 2, False), (4 * p + 2, 2, False)])
        return carry

    def tail(j, rem):
        parts = []
        if rem >= 2:
            parts.append((j, 2, False))
        if rem % 2:
            parts.append((j + rem - 1, 1, False))
        return parts + [(j + rem, 2, True)]

    if isinstance(i, int):
        n_far = max(i - 1, 0)
        lax.fori_loop(0, n_far // 4, far_quad, 0)
        run(tail(n_far - n_far % 4, n_far % 4) if i >= 1 else [(0, 1, True)])
    else:
        n_far = jnp.maximum(i - 1, 0)
        n_quad = lax.shift_right_logical(n_far, 2)
        lax.fori_loop(0, n_quad, far_quad, 0)
        for rem in range(4):
            pl.when((i >= 1) & ((n_far & 3) == rem))(functools.partial(run, tail(4 * n_quad, rem)))
        pl.when(i == 0)(functools.partial(run, [(0, 1, True)]))


def _row_start(j, tk):
    return j * tk if isinstance(j, int) else pl.multiple_of(j * tk, tk)


def _attn_a_kernel(lamv_ref, q_ref, k_ref, vt_ref, bias_ref, gsub_ref, o_ref, m_ref, acc_ref,
                   *, q_tile_index, lam_init):
    tq = q_ref.shape[0]
    tk = vt_ref.shape[2]
    i = pl.program_id(1) if q_tile_index is None else q_tile_index
    weights = _half_masked_weights(q_ref[...], DIFF_HEADS)
    m_ref[...] = jnp.full(m_ref.shape, M_INIT, F32)
    acc_ref[...] = jnp.zeros(acc_ref.shape, F32)

    def scores(j0, width):
        keys = pl.ds(_row_start(j0, tk), width * tk)
        return [_dot_nt(k_ref[keys, h * LANES:(h + 1) * LANES], weights[2 * h * tq:(2 * h + 2) * tq, :])
                for h in range(DIFF_HEADS)]

    def attend(j0, width, near, scored):
        for h in range(DIFF_HEADS):
            s = scored[h].astype(BF16)
            if near:
                b = bias_ref[h] if width == 2 else bias_ref[h, tk:, :]
                s = s + jnp.concatenate([b, b], axis=1)
            m_new, alpha, p = _softmax_step(s, m_ref[h])
            m_ref[h] = m_new
            vt = jnp.concatenate([vt_ref[j0 + w, h * VA_ROWS:(h + 1) * VA_ROWS, :] for w in range(width)], axis=1)
            acc_ref[h] = alpha * acc_ref[h] + _dot(vt, p)

    _run_tiles(i, scores, attend)

    lv = lamv_ref[...]
    lam = (jnp.exp(jnp.sum(lv[0:1] * lv[1:2], axis=1, keepdims=True))
           - jnp.exp(jnp.sum(lv[2:3] * lv[3:4], axis=1, keepdims=True)) + lam_init)
    for h in range(DIFF_HEADS):
        acc = acc_ref[h]
        o = acc[:DIFF_V_DIM, :] / acc[DIFF_V_DIM:DIFF_V_DIM + 1, :]
        d = o[:, :tq] - lam * o[:, tq:]
        y = d * lax.rsqrt(jnp.mean(d * d, axis=0, keepdims=True) + EPS) * gsub_ref[...] * (1.0 - lam_init)
        o_ref[:, h * DIFF_V_DIM:(h + 1) * DIFF_V_DIM] = y.T.astype(BF16)


def _attn_a(lamv, q, k, vt, bias, gsub, *, batch, nq, tq, tkeys, q_tile_index, lam_init):
    tk = vt.shape[2]
    nkt = tkeys // tk
    kern = functools.partial(_attn_a_kernel, q_tile_index=q_tile_index, lam_init=lam_init)
    return pl.pallas_call(
        kern, grid=(batch, nq),
        in_specs=[_const_spec(lamv.shape),
                  pl.BlockSpec((tq, DIFF_QK), lambda b, i: (b * nq + i, 0)),
                  pl.BlockSpec((tkeys, DIFF_QK), lambda b, i: (b, 0)),
                  pl.BlockSpec((nkt, DIFF_HEADS * VA_ROWS, tk), lambda b, i: (b, 0, 0)),
                  _const_spec(bias.shape), _const_spec(gsub.shape)],
        out_specs=pl.BlockSpec((tq, DIFF_WIDTH), lambda b, i: (b * nq + i, 0)),
        out_shape=jax.ShapeDtypeStruct((batch * nq * tq, DIFF_WIDTH), BF16),
        scratch_shapes=[pltpu.VMEM((DIFF_HEADS, 1, 2 * tq), F32),
                        pltpu.VMEM((DIFF_HEADS, VA_ROWS, 2 * tq), F32)],
        compiler_params=_params(2), name="attn_a")(lamv, q, k, vt, bias, gsub)


def _key_to_f32(key):
    return pltpu.bitcast(key ^ ((key >> 31) & 0x7FFFFFFF), F32)


def _attn_b_kernel(iq_ref, ikk_ref, iwt_ref, sq_ref, skk_ref, svt_ref, bias_ref, vis_ref, tri_ref, o_ref,
                   sc_ref, hi_ref, lo_ref, m_ref, acc_ref, tie_ref, *, q_tile_index, n_sel):
    tq = iq_ref.shape[0]
    tk = svt_ref.shape[2]
    static = q_tile_index is not None
    i = q_tile_index if static else pl.program_id(1)
    n_tiles = i + 1
    n_pairs = (n_tiles + 1) // 2 if static else lax.shift_right_logical(n_tiles + 1, 1)
    neg_inf = float("-inf")

    w_idx = _half_masked_weights(iq_ref[...], IDX_HEADS // 2)
    iw = iwt_ref[...]

    def index_scores(j0, width, visible):
        row0 = _row_start(j0, tk)
        rows = width * tk
        rel = _dot_nt(ikk_ref[pl.ds(row0, rows), :], w_idx)
        score = None
        for h in range(IDX_HEADS):
            term = iw[h:h + 1, :] * jnp.maximum(rel[:, h * tq:(h + 1) * tq], 0.0)
            score = term if score is None else score + term
        if visible is not None:
            score = jnp.where(visible > 0.0, score, neg_inf)
        sc_ref[pl.ds(row0, rows), :] = score
        top = pltpu.bitcast(pltpu.bitcast(score, I32) & jnp.int32(-65536), F32)
        hi_ref[pl.ds(row0, rows), :] = top.astype(jnp.bfloat16)

    def idx_pair(p, carry):
        index_scores(2 * p, 2, None)
        return carry

    if static:
        lax.fori_loop(0, i // 2, idx_pair, 0)
        if i % 2:
            index_scores(i - 1, 1, None)
    else:
        lax.fori_loop(0, lax.shift_right_logical(i, 1), idx_pair, 0)
        pl.when((i & 1) == 1)(lambda: index_scores(i - 1, 1, None))
    index_scores(i, 1, vis_ref[...])
    pad0 = _row_start(n_tiles, tk)
    sc_ref[pl.ds(pad0, tk), :] = jnp.full((tk, tq), neg_inf, F32)
    hi_ref[pl.ds(pad0, tk), :] = jnp.full((tk, tq), neg_inf, jnp.bfloat16)

    n_slab = 2 * tk // COUNT_ROWS

    def count(pred):
        def body(pp, acc):
            row0 = pl.multiple_of(pp * (2 * tk), 2 * tk)
            hit = pred(sc_ref[pl.ds(row0, 2 * tk), :], row0).astype(F32)
            return acc + jnp.sum(hit.reshape(n_slab, COUNT_ROWS, tq), axis=0)
        acc = lax.fori_loop(0, n_pairs, body, jnp.zeros((COUNT_ROWS, tq), F32))
        return jnp.sum(acc, axis=0, keepdims=True)

    def count_packed(ref, cand):
        one, zero = jnp.ones((), ref.dtype), jnp.zeros((), ref.dtype)

        def body(pp, acc):
            row0 = pl.multiple_of(pp * (2 * tk), 2 * tk)
            hit = jnp.where(ref[pl.ds(row0, 2 * tk), :] >= cand, one, zero)
            slabs = [hit[r * COUNT_ROWS:(r + 1) * COUNT_ROWS] for r in range(n_slab)]
            while len(slabs) > 1:
                slabs = [a + b for a, b in zip(slabs[0::2], slabs[1::2])]
            return acc + slabs[0]
        acc = lax.fori_loop(0, n_pairs, body, jnp.zeros((COUNT_ROWS, tq), ref.dtype))
        return jnp.sum(acc.astype(F32), axis=0, keepdims=True)

    def top_bit(p, t):
        cand = t + jnp.left_shift(jnp.int32(1), 31 - p)
        cand_bf = pltpu.bitcast(cand ^ ((cand >> 31) & 0x7FFF0000), F32).astype(jnp.bfloat16)
        ok = (count_packed(hi_ref, cand_bf) >= n_sel) | (cand < KEY_NEG_INF)
        return jnp.where(ok, cand, t)

    base = lax.fori_loop(0, 16, top_bit, jnp.full((1, tq), INT_MIN, I32))
    zero_bucket = base == 0

    def settle_zero():
        cnt = count(lambda tile, _: tile >= 0.0)
        return jnp.where(zero_bucket & (cnt < n_sel), jnp.int32(-65536), base)

    base = lax.cond(jnp.max(zero_bucket.astype(I32)) > 0, settle_zero, lambda: base)

    base_hi = base >> 16

    def fill_low(pp, carry):
        row0 = pl.multiple_of(pp * (2 * tk), 2 * tk)
        bits = pltpu.bitcast(sc_ref[pl.ds(row0, 2 * tk), :], I32)
        key = bits ^ ((bits >> 31) & 0x7FFFFFFF)
        bucket = key >> 16
        low = jnp.where(bucket > base_hi, 32767, jnp.where(bucket < base_hi, -32768, (key & 0xFFFF) - 32768))
        lo_ref[pl.ds(row0, 2 * tk), :] = low.astype(jnp.int16)
        return carry

    lax.fori_loop(0, n_pairs, fill_low, 0)

    def low_bit(p, t):
        cand = t + jnp.left_shift(jnp.int32(1), 15 - p)
        cnt = count_packed(lo_ref, (cand - 32768).astype(jnp.int16))
        ok = (cnt >= n_sel) | (base + cand < KEY_NEG_INF)
        return jnp.where(ok, cand, t)

    thr_key = base + lax.fori_loop(0, 16, low_bit, jnp.zeros((1, tq), I32))
    thr = jnp.maximum(_key_to_f32(thr_key), jnp.finfo(F32).min)
    tie_budget = n_sel - count(lambda tile, _: tile > thr)

    w_q = _half_masked_weights(sq_ref[...], DSA_HEADS // 2)
    m_ref[...] = jnp.full(m_ref.shape, M_INIT, F32)
    acc_ref[...] = jnp.zeros(acc_ref.shape, F32)
    tie_ref[...] = jnp.zeros(tie_ref.shape, F32)

    def scores(j0, width):
        row0 = _row_start(j0, tk)
        rows = width * tk
        sc = sc_ref[pl.ds(row0, rows), :]
        tie = sc == thr
        tie_one = jnp.where(tie, 1.0, 0.0).astype(jnp.bfloat16)
        seen = tie_ref[...]
        ranks = []
        for w in range(width):
            ranks.append(_dot(tri_ref[...], tie_one[w * tk:(w + 1) * tk]) + seen)
            seen = ranks[-1][tk - 1:tk, :]
        tie_ref[...] = seen
        sel = (sc > thr) | (tie & (jnp.concatenate(ranks, axis=0) <= tie_budget))
        masked = jnp.where(sel, 0.0, S_MASKED).astype(BF16)
        return masked, _dot_nt(skk_ref[pl.ds(row0, rows), :], w_q)

    def attend(j0, width, near, scored):
        masked, s_all = scored
        parts = []
        for h in range(DSA_HEADS):
            s = s_all[:, h * tq:(h + 1) * tq].astype(BF16) + masked
            if near:
                s = s + (bias_ref[h] if width == 2 else bias_ref[h, tk:, :])
            parts.append(s)
        m_new, alpha, p = _softmax_step(jnp.concatenate(parts, axis=1), m_ref[...])
        vt = jnp.concatenate([svt_ref[j0 + w] for w in range(width)], axis=1)
        acc_ref[...] = alpha * acc_ref[...] + _dot(vt, p)
        m_ref[...] = m_new

    _run_tiles(i, scores, attend)

    acc = acc_ref[...]
    o = acc[:HEAD_DIM, :] / acc[HEAD_DIM:HEAD_DIM + 1, :]
    for pr in range(DSA_HEADS // 2):
        pair = jnp.concatenate([o[:, 2 * pr * tq:(2 * pr + 1) * tq], o[:, (2 * pr + 1) * tq:(2 * pr + 2) * tq]],
                               axis=0)
        o_ref[:, pr * LANES:(pr + 1) * LANES] = pair.T.astype(BF16)


def _attn_b(p, bias, vis, *, batch, nq, tq, tkeys, q_tile_index, n_sel):
    tk = p["svt"].shape[2]
    nkt = tkeys // tk
    assert (tkeys + tk) // COUNT_ROWS <= 256, "bf16 partial counts must stay exact"
    kern = functools.partial(_attn_b_kernel, q_tile_index=q_tile_index, n_sel=n_sel)
    tri = jnp.tril(jnp.ones((tk, tk), jnp.bfloat16))
    qrow = lambda b, i: (b * nq + i, 0)
    krow = lambda b, i: (b, 0)
    return pl.pallas_call(
        kern, grid=(batch, nq),
        in_specs=[pl.BlockSpec((tq, IDX_HEADS * IDX_DIM), qrow),
                  pl.BlockSpec((tkeys, LANES), krow),
                  pl.BlockSpec((8, tq), lambda b, i: (0, b * nq + i)),
                  pl.BlockSpec((tq, DSA_WIDTH), qrow),
                  pl.BlockSpec((tkeys, LANES), krow),
                  pl.BlockSpec((nkt, VB_ROWS, tk), lambda b, i: (b, 0, 0)),
                  _const_spec(bias.shape), _const_spec(vis.shape), _const_spec(tri.shape)],
        out_specs=pl.BlockSpec((tq, DSA_WIDTH), qrow),
        out_shape=jax.ShapeDtypeStruct((batch * nq * tq, DSA_WIDTH), BF16),
        scratch_shapes=[pltpu.VMEM((tkeys + tk, tq), F32),
                        pltpu.VMEM((tkeys + tk, tq), jnp.bfloat16),
                        pltpu.VMEM((tkeys + tk, tq), jnp.int16),
                        pltpu.VMEM((1, DSA_HEADS * tq), F32),
                        pltpu.VMEM((VB_ROWS, DSA_HEADS * tq), F32),
                        pltpu.VMEM((1, tq), F32)],
        compiler_params=_params(2), name="attn_b")(
            p["iq"], p["ikk"], p["iwt"], p["sq"], p["skk"], p["svt"], bias, vis, tri)


def _t5_bucket(rel):
    half = NUM_BUCKETS // 2
    exact = half // 2
    n = jnp.abs(rel)
    logn = jnp.log(jnp.maximum(n, 1).astype(F32) / exact)
    large = exact + (logn / math.log(MAX_DISTANCE / exact) * (half - exact)).astype(I32)
    large = jnp.minimum(large, half - 1)
    return jnp.where(rel > 0, half, 0) + jnp.where(n < exact, n, large)


def _bias_lookup(rel_bias, bucket):
    out = jnp.zeros((rel_bias.shape[1],) + bucket.shape, F32)
    for b in range(NUM_BUCKETS):
        out = out + jnp.where(bucket[None] == b, rel_bias[b].astype(F32)[:, None, None], 0.0)
    return out


def _bias_tiles(rel_bias, tk, tq, q_valid, k_valid_diag):
    kk = jnp.arange(tk, dtype=I32)[:, None]
    qq = jnp.arange(tq, dtype=I32)[None, :]
    far = _bias_lookup(rel_bias, _t5_bucket(jnp.full((1, 1), -(2 * MAX_DISTANCE), I32)))
    tiles = [(_bias_lookup(rel_bias, _t5_bucket(kk + d - qq)) - far) * LOG2E for d in (-tk, 0)]
    vis = ((kk // CHUNK) <= (qq // CHUNK)) & (kk < k_valid_diag) | (qq >= q_valid)
    tiles[1] = jnp.where(vis[None], tiles[1], M_INIT)
    return jnp.concatenate(tiles, axis=1).astype(BF16), vis.astype(F32)


def _token_mix(p, rel_bias, lamv, gsub, *, batch, nq, tq, tkeys, q_tile_index, q_valid, k_valid_diag,
               n_sel, lam_init):
    bias, vis = _bias_tiles(rel_bias, KEY_TILE, tq, q_valid, k_valid_diag)
    gsub_b = jnp.broadcast_to(gsub.astype(F32)[:, None], (DIFF_V_DIM, tq))
    o_a = _attn_a(lamv, p["dq"], p["dk"], p["dvt"], bias[:DIFF_HEADS], gsub_b, batch=batch, nq=nq, tq=tq,
                  tkeys=tkeys, q_tile_index=q_tile_index, lam_init=lam_init)
    o_b = _attn_b(p, bias[DIFF_HEADS:], vis, batch=batch, nq=nq, tq=tq, tkeys=tkeys,
                  q_tile_index=q_tile_index, n_sel=n_sel)
    return o_a, o_b


def _pad_rows(a, rows):
    return jnp.pad(a, ((0, 0), (0, rows - a.shape[1])) + ((0, 0),) * (a.ndim - 2))


def kernel(x_prompt, x_sample, cache_diff_k, cache_diff_v, cache_dsa_k, cache_dsa_v, cache_idx_k, g_ffn1, w1_gate, w1_up, w1_down, g_mix, w_in, g_dq, g_dk, g_sq, g_sk, lam_q1, lam_k1, lam_q2, lam_k2, g_subln, w_out, g_ffn2, w2_gate, w2_up, w2_down, rel_bias):
    depth = g_ffn1.shape[0]
    assert depth == 1, "one layer"
    l = 0
    batch, seq, d_model = x_prompt.shape
    sbatch, sseq, _ = x_sample.shape
    past = cache_diff_k.shape[2]
    tk = KEY_TILE
    assert seq % Q_TILE == 0 and Q_TILE % CHUNK == 0 and Q_TILE == tk
    assert past % tk == 0 and past % CHUNK == 0 and sseq <= CHUNK and sseq % 8 == 0
    lam_init = 0.8 - 0.6 * math.exp(-0.3 * l)

    bf = lambda w: w.astype(BF16)
    row = lambda g: g.astype(F32)[None, :]
    w1 = (row(g_ffn1[l]), bf(w1_gate[l]), bf(w1_up[l]), bf(w1_down[l]))
    w2 = (row(g_ffn2[l]), bf(w2_gate[l]), bf(w2_up[l]), bf(w2_down[l]))
    w_in_p = jnp.pad(bf(w_in[l]), ((0, 0), (0, IN_COLS_PAD - IN_COLS)))
    wo_a, wo_b = bf(w_out[l][:DIFF_WIDTH]), bf(w_out[l][DIFF_WIDTH:])
    lamv = jnp.pad(jnp.stack([lam_q1[l], lam_k1[l], lam_q2[l], lam_k2[l]]).astype(F32), ((0, 4), (0, LANES - HEAD_DIM)))
    proj = functools.partial(_proj, g_mix=row(g_mix[l]), w_in=w_in_p, g_dq=g_dq[l], g_dk=g_dk[l], g_sq=g_sq[l],
                             g_sk=g_sk[l])

    xp1 = _ffn(x_prompt.reshape(batch * seq, d_model), *w1)
    pp = proj(xp1)
    oa_p, ob_p = _token_mix(pp, rel_bias, lamv, g_subln[l], batch=batch, nq=seq // Q_TILE, tq=Q_TILE, tkeys=seq,
                            q_tile_index=None, q_valid=Q_TILE, k_valid_diag=tk,
                            n_sel=min(TOPK_MAX, seq // 4), lam_init=lam_init)
    yp = _out_ffn(xp1, oa_p, ob_p, wo_a, wo_b, *w2).reshape(batch, seq, d_model)

    ns = sbatch * sseq
    xs1 = _ffn(x_sample.reshape(ns, d_model), *w1)
    ps = proj(xs1)
    tq_s = LANES
    keys_s = past + tk
    per_b = lambda a: a.reshape(sbatch, sseq, a.shape[-1])

    def q_rows(a):
        return _pad_rows(per_b(a), tq_s).reshape(sbatch * tq_s, a.shape[-1])

    def k_rows(cache, new):
        both = jnp.concatenate([cache.astype(BF16), per_b(new)], axis=1)
        return _pad_rows(both, keys_s).reshape(sbatch * keys_s, new.shape[-1])

    def v_tiles(cache, new_t, rows, heads):
        rv = rows - ONES_ROWS
        c = cache.astype(BF16).reshape(sbatch, past // tk, tk, heads, rv)
        c = jnp.transpose(c, (0, 1, 3, 4, 2))
        c = jnp.concatenate([c, jnp.ones((sbatch, past // tk, heads, ONES_ROWS, tk), BF16)], axis=3)
        c = c.reshape(sbatch, past // tk, heads * rows, tk)
        n = new_t.reshape(heads * rows, sbatch, sseq)
        n = jnp.pad(jnp.transpose(n, (1, 0, 2)), ((0, 0), (0, 0), (0, tk - sseq)))[:, None]
        return jnp.concatenate([c, n], axis=1).reshape(sbatch * (past // tk + 1), heads * rows, tk)

    dup = lambda c: jnp.concatenate([c, c], axis=-1)
    iwt_s = jnp.pad(ps["iwt"].reshape(8, sbatch, sseq), ((0, 0), (0, 0), (0, tq_s - sseq))).reshape(8, sbatch * tq_s)
    psa = {
        "dq": q_rows(ps["dq"]), "sq": q_rows(ps["sq"]), "iq": q_rows(ps["iq"]), "iwt": iwt_s,
        "dk": k_rows(cache_diff_k[l].reshape(sbatch, past, DIFF_QK), ps["dk"]),
        "skk": k_rows(dup(cache_dsa_k[l]), ps["skk"]),
        "ikk": k_rows(dup(cache_idx_k[l]), ps["ikk"]),
        "dvt": v_tiles(cache_diff_v[l].reshape(sbatch, past, DIFF_WIDTH), ps["dvt"], VA_ROWS, DIFF_HEADS),
        "svt": v_tiles(cache_dsa_v[l], ps["svt"], VB_ROWS, 1),
    }
    oa_s, ob_s = _token_mix(psa, rel_bias, lamv, g_subln[l], batch=sbatch, nq=1, tq=tq_s, tkeys=keys_s,
                            q_tile_index=past // tk, q_valid=sseq, k_valid_diag=sseq,
                            n_sel=min(TOPK_MAX, (past + sseq) // 4), lam_init=lam_init)
    take = lambda o: o.reshape(sbatch, tq_s, o.shape[-1])[:, :sseq].reshape(ns, o.shape[-1])
    ys = _out_ffn(xs1, take(oa_s), take(ob_s), wo_a, wo_b, *w2).reshape(sbatch, sseq, d_model)

    def caches(p, b, t):
        return (p["dk_c"].reshape(depth, b, t, DIFF_HEADS, 2, HEAD_DIM),
                p["dv_c"].reshape(depth, b, t, DIFF_HEADS, DIFF_V_DIM),
                p["sk_c"].reshape(depth, b, t, HEAD_DIM), p["sv_c"].reshape(depth, b, t, HEAD_DIM),
                p["ik_c"].reshape(depth, b, t, IDX_DIM))

    return (yp, ys) + caches(pp, batch, seq) + caches(ps, sbatch, sseq)
```

```python
import functools
import math

import jax
import jax.numpy as jnp
from jax import lax
from jax.experimental import pallas as pl
from jax.experimental.pallas import tpu as pltpu

F32 = jnp.float32
BF16 = jnp.bfloat16
I32 = jnp.int32

CHUNK = 64
HEAD_DIM = 64
DIFF_HEADS = 4
DIFF_V_DIM = 2 * HEAD_DIM
DSA_HEADS = 8
IDX_HEADS = 4
IDX_DIM = 64
TOPK_MAX = 256
NUM_BUCKETS = 32
MAX_DISTANCE = 128
EPS = 1e-6
IDX_SCALE = (IDX_HEADS ** -0.5) * (IDX_DIM ** -0.5)
QK_SCALE = HEAD_DIM ** -0.5
LOG2E = math.log2(math.e)

DIFF_QK = DIFF_HEADS * 2 * HEAD_DIM
DIFF_WIDTH = DIFF_HEADS * DIFF_V_DIM
DSA_WIDTH = DSA_HEADS * HEAD_DIM
IN_COLS = 3 * DIFF_QK + DSA_WIDTH + 2 * HEAD_DIM + IDX_HEADS * IDX_DIM + IDX_DIM + IDX_HEADS

LANES = 128
BF16_SUBLANES = 16
MXU_COLS = 256
VMEM_LIMIT_BYTES = 60000 * 1024

IN_COLS_PAD = -(-IN_COLS // MXU_COLS) * MXU_COLS
ONES_ROWS = BF16_SUBLANES
VA_ROWS = DIFF_V_DIM + ONES_ROWS
VB_ROWS = HEAD_DIM + ONES_ROWS

KEY_TILE = 256
Q_TILE = 256
M_INIT = -(2.0 ** 100)
S_MASKED = -2e30
INT_MIN = -(2 ** 31)
KEY_NEG_INF = INT_MIN + 0x7FFFFF
COUNT_ROWS = 32

_C_DQ, _C_DK, _C_DV, _C_SQ = 0, DIFF_QK, 2 * DIFF_QK, 3 * DIFF_QK
_C_SKV = _C_SQ + DSA_WIDTH
_C_IQ = _C_SKV + 2 * HEAD_DIM
_C_IKW = _C_IQ + IDX_HEADS * IDX_DIM


def _params(n_axes):
    return pltpu.CompilerParams(dimension_semantics=("arbitrary",) * n_axes,
                                vmem_limit_bytes=VMEM_LIMIT_BYTES)


def _const_spec(shape):
    nd = len(shape)
    return pl.BlockSpec(shape, lambda *_: (0,) * nd, pipeline_mode=pl.Buffered(1))


def _rms(x, g):
    return x * lax.rsqrt(jnp.mean(x * x, axis=-1, keepdims=True) + EPS) * g


def _dot(a, b):
    return jnp.dot(a, b, preferred_element_type=F32)


def _dot_nt(a, b):
    return lax.dot_general(a, b, (((1,), (1,)), ((), ())), preferred_element_type=F32)


def _ff_chunks(d_ff):
    step = 4 * MXU_COLS
    return [(s, min(step, d_ff - s)) for s in range(0, d_ff, step)]


def _swiglu_half(x, g_ref, wg_ref, wu_ref, wd_ref):
    h = _rms(x, g_ref[...]).astype(BF16)
    acc = None
    for start, width in _ff_chunks(wg_ref.shape[1]):
        gate = _dot(h, wg_ref[:, start:start + width])
        up = _dot(h, wu_ref[:, start:start + width])
        act = (gate * (1.0 / (1.0 + jnp.exp(-gate))) * up).astype(BF16)
        part = _dot(act, wd_ref[start:start + width, :])
        acc = part if acc is None else acc + part
    return x + 0.5 * acc


def _ffn_kernel(x_ref, g_ref, wg_ref, wu_ref, wd_ref, o_ref):
    o_ref[...] = _swiglu_half(x_ref[...], g_ref, wg_ref, wu_ref, wd_ref)


def _out_ffn_kernel(x_ref, oa_ref, ob_ref, woa_ref, wob_ref, g_ref, wg_ref, wu_ref, wd_ref, o_ref):
    x = x_ref[...] + _dot(oa_ref[...], woa_ref[...]) + _dot(ob_ref[...], wob_ref[...])
    o_ref[...] = _swiglu_half(x, g_ref, wg_ref, wu_ref, wd_ref)


def _row_tile(n):
    for tm in (512, 256, 128, 64, 32, 16, 8):
        if n % tm == 0:
            return tm
    raise ValueError(f"row count {n} is not a multiple of 8")


def _ffn(x, g, wg, wu, wd):
    n, d = x.shape
    tm = _row_tile(n)
    row = pl.BlockSpec((tm, d), lambda i: (i, 0))
    return pl.pallas_call(
        _ffn_kernel, grid=(n // tm,),
        in_specs=[row, _const_spec(g.shape), _const_spec(wg.shape), _const_spec(wu.shape), _const_spec(wd.shape)],
        out_specs=row, out_shape=jax.ShapeDtypeStruct((n, d), F32),
        compiler_params=_params(1), name="ffn")(x, g, wg, wu, wd)


def _out_ffn(x, oa, ob, woa, wob, g, wg, wu, wd):
    n, d = x.shape
    tm = _row_tile(n)
    row = pl.BlockSpec((tm, d), lambda i: (i, 0))
    half = pl.BlockSpec((tm, oa.shape[1]), lambda i: (i, 0))
    consts = [woa, wob, g, wg, wu, wd]
    return pl.pallas_call(
        _out_ffn_kernel, grid=(n // tm,),
        in_specs=[row, half, half] + [_const_spec(c.shape) for c in consts],
        out_specs=row, out_shape=jax.ShapeDtypeStruct((n, d), F32),
        compiler_params=_params(1), name="out_ffn")(x, oa, ob, *consts)


def _dup_halves(t, first):
    lane = lax.broadcasted_iota(I32, t.shape, 1)
    rolled = pltpu.roll(t, HEAD_DIM, axis=1)
    if first:
        return jnp.where(lane < HEAD_DIM, t, rolled)
    return jnp.where(lane < HEAD_DIM, rolled, t)


def _proj_kernel(x_ref, g_ref, win_ref, gdq_ref, gdk_ref, gsq_ref, gsk_ref, g512_ref, g128_ref,
                 dkc_ref, dvc_ref, skc_ref, svc_ref, ikc_ref,
                 dq_ref, dkb_ref, dvt_ref, sq_ref, skk_ref, svt_ref, iq_ref, ikk_ref, iwt_ref):
    h = _rms(x_ref[...], g_ref[...]).astype(BF16)
    z = _dot(h, win_ref[...])
    n_sub, _, t_sub = dvt_ref.shape

    def headnorm(seg, ones_ref, gain):
        ss = _dot((seg * seg).astype(BF16), ones_ref[...])
        return seg * lax.rsqrt(ss * (1.0 / HEAD_DIM) + EPS) * gain

    dq_ref[...] = headnorm(z[:, _C_DQ:_C_DQ + DIFF_QK], g512_ref, gdq_ref[...]).astype(BF16)
    dk = headnorm(z[:, _C_DK:_C_DK + DIFF_QK], g512_ref, gdk_ref[...])
    dkc_ref[0] = dk.T
    dkb_ref[...] = dk.astype(BF16)
    sq_ref[...] = headnorm(z[:, _C_SQ:_C_SQ + DSA_WIDTH], g512_ref, gsq_ref[...]).astype(BF16)

    dv = z[:, _C_DV:_C_DV + DIFF_WIDTH]
    dvc_ref[...] = dv.reshape(dvc_ref.shape)
    ones_a = jnp.ones((ONES_ROWS, t_sub), BF16)
    for s in range(n_sub):
        for hh in range(DIFF_HEADS):
            blk = dv[s * t_sub:(s + 1) * t_sub, hh * DIFF_V_DIM:(hh + 1) * DIFF_V_DIM]
            dvt_ref[s, hh * VA_ROWS:hh * VA_ROWS + DIFF_V_DIM, :] = blk.T.astype(BF16)
            dvt_ref[s, hh * VA_ROWS + DIFF_V_DIM:(hh + 1) * VA_ROWS, :] = ones_a

    skv = z[:, _C_SKV:_C_SKV + LANES]
    skn = headnorm(skv, g128_ref, gsk_ref[...])
    skk = _dup_halves(skn, first=True)
    svv = _dup_halves(skv, first=False)
    skc_ref[0] = skk.T[:HEAD_DIM, :]
    sv_t = svv.T[:HEAD_DIM, :]
    svc_ref[0] = sv_t
    skk_ref[...] = skk.astype(BF16)
    for s in range(n_sub):
        svt_ref[s, 0:HEAD_DIM, :] = sv_t[:, s * t_sub:(s + 1) * t_sub].astype(BF16)
        svt_ref[s, HEAD_DIM:VB_ROWS, :] = ones_a

    iq_ref[...] = z[:, _C_IQ:_C_IQ + IDX_HEADS * IDX_DIM].astype(BF16)
    ikw = z[:, _C_IKW:_C_IKW + LANES]
    ikw_t = ikw.T
    ikc_ref[0] = ikw_t[:HEAD_DIM, :]
    ikk_ref[...] = _dup_halves(ikw, first=True).astype(BF16)
    iwt_ref[...] = ikw_t[HEAD_DIM:HEAD_DIM + 8, :] * IDX_SCALE


def _proj(x1, g_mix, w_in, g_dq, g_dk, g_sq, g_sk, *, batch):
    n, d = x1.shape
    seq = n // batch
    tm = _row_tile(seq)
    per_b = seq // tm
    t_sub = min(KEY_TILE, tm)
    n_sub = tm // t_sub
    nt = n // t_sub

    def tile8(g, scale):
        return (jnp.tile(g.astype(F32), DIFF_QK // HEAD_DIM) * scale)[None, :]

    gsk_pad = jnp.concatenate([g_sk.astype(F32), jnp.ones((HEAD_DIM,), F32)])[None, :]
    grp = jnp.arange(DIFF_QK) // HEAD_DIM
    ones512 = (grp[:, None] == grp[None, :]).astype(BF16)
    ones128 = ones512[:LANES, :LANES]
    consts = [g_mix, w_in, tile8(g_dq, QK_SCALE * LOG2E), tile8(g_dk, 1.0), tile8(g_sq, QK_SCALE * LOG2E), gsk_pad,
              ones512, ones128]

    def rows(width):
        return pl.BlockSpec((tm, width), lambda i: (i, 0))

    def tposed(nrows):
        return pl.BlockSpec((n_sub, nrows, t_sub), lambda i: (i, 0, 0))

    def feature_major(nrows):
        return pl.BlockSpec((1, nrows, tm), lambda i: (i // per_b, 0, i % per_b))

    out_shapes = [
        jax.ShapeDtypeStruct((batch, DIFF_QK, seq), F32),
        jax.ShapeDtypeStruct((1, batch, seq, DIFF_HEADS, DIFF_V_DIM), F32),
        jax.ShapeDtypeStruct((batch, HEAD_DIM, seq), F32), jax.ShapeDtypeStruct((batch, HEAD_DIM, seq), F32),
        jax.ShapeDtypeStruct((batch, IDX_DIM, seq), F32),
        jax.ShapeDtypeStruct((n, DIFF_QK), BF16), jax.ShapeDtypeStruct((n, DIFF_QK), BF16),
        jax.ShapeDtypeStruct((nt, DIFF_HEADS * VA_ROWS, t_sub), BF16),
        jax.ShapeDtypeStruct((n, DSA_WIDTH), BF16), jax.ShapeDtypeStruct((n, LANES), BF16),
        jax.ShapeDtypeStruct((nt, VB_ROWS, t_sub), BF16),
        jax.ShapeDtypeStruct((n, IDX_HEADS * IDX_DIM), BF16), jax.ShapeDtypeStruct((n, LANES), BF16),
        jax.ShapeDtypeStruct((8, n), F32),
    ]
    out_specs = [feature_major(DIFF_QK),
                 pl.BlockSpec((1, 1, tm, DIFF_HEADS, DIFF_V_DIM), lambda i: (0, i // per_b, i % per_b, 0, 0)),
                 feature_major(HEAD_DIM), feature_major(HEAD_DIM), feature_major(IDX_DIM),
                 rows(DIFF_QK), rows(DIFF_QK), tposed(DIFF_HEADS * VA_ROWS),
                 rows(DSA_WIDTH), rows(LANES), tposed(VB_ROWS),
                 rows(IDX_HEADS * IDX_DIM), rows(LANES), pl.BlockSpec((8, tm), lambda i: (0, i))]
    outs = pl.pallas_call(
        _proj_kernel, grid=(n // tm,),
        in_specs=[rows(d)] + [_const_spec(c.shape) for c in consts],
        out_specs=out_specs, out_shape=out_shapes,
        compiler_params=_params(1), name="proj")(x1, *consts)
    names = ("dk_c", "dv_c", "sk_c", "sv_c", "ik_c", "dq", "dk", "dvt", "sq", "skk", "svt", "iq", "ikk", "iwt")
    return dict(zip(names, outs))


def _half_masked_weights(q_tile, n_pairs):
    lane = lax.broadcasted_iota(I32, (q_tile.shape[0], LANES), 1)
    parts = []
    for pr in range(n_pairs):
        qp = q_tile[:, pr * LANES:(pr + 1) * LANES].astype(F32)
        parts.append(jnp.where(lane < HEAD_DIM, qp, 0.0).astype(q_tile.dtype))
        parts.append(jnp.where(lane >= HEAD_DIM, qp, 0.0).astype(q_tile.dtype))
    return jnp.concatenate(parts, axis=0)


def _softmax_step(s, m_old):
    m_new = jnp.maximum(m_old, jnp.max(s, axis=0, keepdims=True).astype(F32))
    alpha = jnp.exp2(m_old - m_new)
    p = jnp.exp2(s - m_new.astype(s.dtype))
    return m_new, alpha, p


def _run_tiles(i, scores, attend):
    def run(parts):
        pending = [scores(j0, width) for j0, width, _ in parts]
        for (j0, width, near), s in zip(parts, pending):
            attend(j0, width, near, s)

    def far_quad(p, carry):
        run([(4 * p, 2, False), (4 * p + 2, 2, False)])
        return carry

    def tail(j, rem):
        parts = []
        if rem >= 2:
            parts.append((j, 2, False))
        if rem % 2:
            parts.append((j + rem - 1, 1, False))
        return parts + [(j + rem, 2, True)]

    if isinstance(i, int):
        n_far = max(i - 1, 0)
        lax.fori_loop(0, n_far // 4, far_quad, 0)
        run(tail(n_far - n_far % 4, n_far % 4) if i >= 1 else [(0, 1, True)])
    else:
        n_far = jnp.maximum(i - 1, 0)
        n_quad = lax.shift_right_logical(n_far, 2)
        lax.fori_loop(0, n_quad, far_quad, 0)
        for rem in range(4):
            pl.when((i >= 1) & ((n_far & 3) == rem))(functools.partial(run, tail(4 * n_quad, rem)))
        pl.when(i == 0)(functools.partial(run, [(0, 1, True)]))


def _row_start(j, tk):
    return j * tk if isinstance(j, int) else pl.multiple_of(j * tk, tk)


def _attn_a_kernel(lamv_ref, q_ref, k_ref, vt_ref, bias_ref, gsub_ref, o_ref, m_ref, acc_ref,
                   *, q_tile_index, lam_init):
    tq = q_ref.shape[0]
    tk = vt_ref.shape[2]
    i = pl.program_id(1) if q_tile_index is None else q_tile_index
    weights = _half_masked_weights(q_ref[...], DIFF_HEADS)
    m_ref[...] = jnp.full(m_ref.shape, M_INIT, F32)
    acc_ref[...] = jnp.zeros(acc_ref.shape, F32)

    def scores(j0, width):
        keys = pl.ds(_row_start(j0, tk), width * tk)
        return [_dot_nt(k_ref[keys, h * LANES:(h + 1) * LANES], weights[2 * h * tq:(2 * h + 2) * tq, :])
                for h in range(DIFF_HEADS)]

    def attend(j0, width, near, scored):
        for h in range(DIFF_HEADS):
            s = scored[h].astype(BF16)
            if near:
                b = bias_ref[h] if width == 2 else bias_ref[h, tk:, :]
                s = s + jnp.concatenate([b, b], axis=1)
            m_new, alpha, p = _softmax_step(s, m_ref[h])
            m_ref[h] = m_new
            vt = jnp.concatenate([vt_ref[j0 + w, h * VA_ROWS:(h + 1) * VA_ROWS, :] for w in range(width)], axis=1)
            acc_ref[h] = alpha * acc_ref[h] + _dot(vt, p)

    _run_tiles(i, scores, attend)

    lv = lamv_ref[...]
    lam = (jnp.exp(jnp.sum(lv[0:1] * lv[1:2], axis=1, keepdims=True))
           - jnp.exp(jnp.sum(lv[2:3] * lv[3:4], axis=1, keepdims=True)) + lam_init)
    for h in range(DIFF_HEADS):
        acc = acc_ref[h]
        o = acc[:DIFF_V_DIM, :] / acc[DIFF_V_DIM:DIFF_V_DIM + 1, :]
        d = o[:, :tq] - lam * o[:, tq:]
        y = d * lax.rsqrt(jnp.mean(d * d, axis=0, keepdims=True) + EPS) * gsub_ref[...] * (1.0 - lam_init)
        o_ref[:, h * DIFF_V_DIM:(h + 1) * DIFF_V_DIM] = y.T.astype(BF16)


def _attn_a(lamv, q, k, vt, bias, gsub, *, batch, nq, tq, tkeys, q_tile_index, lam_init):
    tk = vt.shape[2]
    nkt = tkeys // tk
    kern = functools.partial(_attn_a_kernel, q_tile_index=q_tile_index, lam_init=lam_init)
    return pl.pallas_call(
        kern, grid=(batch, nq),
        in_specs=[_const_spec(lamv.shape),
                  pl.BlockSpec((tq, DIFF_QK), lambda b, i: (b * nq + i, 0)),
                  pl.BlockSpec((tkeys, DIFF_QK), lambda b, i: (b, 0)),
                  pl.BlockSpec((nkt, DIFF_HEADS * VA_ROWS, tk), lambda b, i: (b, 0, 0)),
                  _const_spec(bias.shape), _const_spec(gsub.shape)],
        out_specs=pl.BlockSpec((tq, DIFF_WIDTH), lambda b, i: (b * nq + i, 0)),
        out_shape=jax.ShapeDtypeStruct((batch * nq * tq, DIFF_WIDTH), BF16),
        scratch_shapes=[pltpu.VMEM((DIFF_HEADS, 1, 2 * tq), F32),
                        pltpu.VMEM((DIFF_HEADS, VA_ROWS, 2 * tq), F32)],
        compiler_params=_params(2), name="attn_a")(lamv, q, k, vt, bias, gsub)


def _key_to_f32(key):
    return pltpu.bitcast(key ^ ((key >> 31) & 0x7FFFFFFF), F32)


def _attn_b_kernel(iq_ref, ikk_ref, iwt_ref, sq_ref, skk_ref, svt_ref, bias_ref, vis_ref, tri_ref, o_ref,
                   sc_ref, hi_ref, lo_ref, m_ref, acc_ref, tie_ref, *, q_tile_index, n_sel):
    tq = iq_ref.shape[0]
    tk = svt_ref.shape[2]
    static = q_tile_index is not None
    i = q_tile_index if static else pl.program_id(1)
    n_tiles = i + 1
    n_pairs = (n_tiles + 1) // 2 if static else lax.shift_right_logical(n_tiles + 1, 1)
    neg_inf = float("-inf")

    w_idx = _half_masked_weights(iq_ref[...], IDX_HEADS // 2)
    iw = iwt_ref[...]

    def index_scores(j0, width, visible):
        row0 = _row_start(j0, tk)
        rows = width * tk
        rel = _dot_nt(ikk_ref[pl.ds(row0, rows), :], w_idx)
        score = None
        for h in range(IDX_HEADS):
            term = iw[h:h + 1, :] * jnp.maximum(rel[:, h * tq:(h + 1) * tq], 0.0)
            score = term if score is None else score + term
        if visible is not None:
            score = jnp.where(visible > 0.0, score, neg_inf)
        sc_ref[pl.ds(row0, rows), :] = score
        top = pltpu.bitcast(pltpu.bitcast(score, I32) & jnp.int32(-65536), F32)
        hi_ref[pl.ds(row0, rows), :] = top.astype(jnp.bfloat16)

    def idx_pair(p, carry):
        index_scores(2 * p, 2, None)
        return carry

    if static:
        lax.fori_loop(0, i // 2, idx_pair, 0)
        if i % 2:
            index_scores(i - 1, 1, None)
    else:
        lax.fori_loop(0, lax.shift_right_logical(i, 1), idx_pair, 0)
        pl.when((i & 1) == 1)(lambda: index_scores(i - 1, 1, None))
    index_scores(i, 1, vis_ref[...])
    pad0 = _row_start(n_tiles, tk)
    sc_ref[pl.ds(pad0, tk), :] = jnp.full((tk, tq), neg_inf, F32)
    hi_ref[pl.ds(pad0, tk), :] = jnp.full((tk, tq), neg_inf, jnp.bfloat16)

    n_slab = 2 * tk // COUNT_ROWS

    def count(pred):
        def body(pp, acc):
            row0 = pl.multiple_of(pp * (2 * tk), 2 * tk)
            hit = pred(sc_ref[pl.ds(row0, 2 * tk), :], row0).astype(F32)
            return acc + jnp.sum(hit.reshape(n_slab, COUNT_ROWS, tq), axis=0)
        acc = lax.fori_loop(0, n_pairs, body, jnp.zeros((COUNT_ROWS, tq), F32))
        return jnp.sum(acc, axis=0, keepdims=True)

    def count_packed(ref, cand):
        one, zero = jnp.ones((), ref.dtype), jnp.zeros((), ref.dtype)

        def body(pp, acc):
            row0 = pl.multiple_of(pp * (2 * tk), 2 * tk)
            hit = jnp.where(ref[pl.ds(row0, 2 * tk), :] >= cand, one, zero)
            slabs = [hit[r * COUNT_ROWS:(r + 1) * COUNT_ROWS] for r in range(n_slab)]
            while len(slabs) > 1:
                slabs = [a + b for a, b in zip(slabs[0::2], slabs[1::2])]
            return acc + slabs[0]
        acc = lax.fori_loop(0, n_pairs, body, jnp.zeros((COUNT_ROWS, tq), ref.dtype))
        return jnp.sum(acc.astype(F32), axis=0, keepdims=True)

    def top_bit(p, t):
        cand = t + jnp.left_shift(jnp.int32(1), 31 - p)
        cand_bf = pltpu.bitcast(cand ^ ((cand >> 31) & 0x7FFF0000), F32).astype(jnp.bfloat16)
        ok = (count_packed(hi_ref, cand_bf) >= n_sel) | (cand < KEY_NEG_INF)
        return jnp.where(ok, cand, t)

    base = lax.fori_loop(0, 16, top_bit, jnp.full((1, tq), INT_MIN, I32))
    zero_bucket = base == 0

    def settle_zero():
        cnt = count(lambda tile, _: tile >= 0.0)
        return jnp.where(zero_bucket & (cnt < n_sel), jnp.int32(-65536), base)

    base = lax.cond(jnp.max(zero_bucket.astype(I32)) > 0, settle_zero, lambda: base)

    base_hi = base >> 16

    def fill_low(pp, carry):
        row0 = pl.multiple_of(pp * (2 * tk), 2 * tk)
        bits = pltpu.bitcast(sc_ref[pl.ds(row0, 2 * tk), :], I32)
        key = bits ^ ((bits >> 31) & 0x7FFFFFFF)
        bucket = key >> 16
        low = jnp.where(bucket > base_hi, 32767, jnp.where(bucket < base_hi, -32768, (key & 0xFFFF) - 32768))
        lo_ref[pl.ds(row0, 2 * tk), :] = low.astype(jnp.int16)
        return carry

    lax.fori_loop(0, n_pairs, fill_low, 0)

    def low_bit(p, t):
        cand = t + jnp.left_shift(jnp.int32(1), 15 - p)
        cnt = count_packed(lo_ref, (cand - 32768).astype(jnp.int16))
        ok = (cnt >= n_sel) | (base + cand < KEY_NEG_INF)
        return jnp.where(ok, cand, t)

    thr_key = base + lax.fori_loop(0, 16, low_bit, jnp.zeros((1, tq), I32))
    thr = jnp.maximum(_key_to_f32(thr_key), jnp.finfo(F32).min)
    tie_budget = n_sel - count(lambda tile, _: tile > thr)

    w_q = _half_masked_weights(sq_ref[...], DSA_HEADS // 2)
    m_ref[...] = jnp.full(m_ref.shape, M_INIT, F32)
    acc_ref[...] = jnp.zeros(acc_ref.shape, F32)
    tie_ref[...] = jnp.zeros(tie_ref.shape, F32)

    def scores(j0, width):
        row0 = _row_start(j0, tk)
        rows = width * tk
        sc = sc_ref[pl.ds(row0, rows), :]
        tie = sc == thr
        tie_one = jnp.where(tie, 1.0, 0.0).astype(jnp.bfloat16)
        seen = tie_ref[...]
        ranks = []
        for w in range(width):
            ranks.append(_dot(tri_ref[...], tie_one[w * tk:(w + 1) * tk]) + seen)
            seen = ranks[-1][tk - 1:tk, :]
        tie_ref[...] = seen
        sel = (sc > thr) | (tie & (jnp.concatenate(ranks, axis=0) <= tie_budget))
        masked = jnp.where(sel, 0.0, S_MASKED).astype(BF16)
        return masked, _dot_nt(skk_ref[pl.ds(row0, rows), :], w_q)

    def attend(j0, width, near, scored):
        masked, s_all = scored
        parts = []
        for h in range(DSA_HEADS):
            s = s_all[:, h * tq:(h + 1) * tq].astype(BF16) + masked
            if near:
                s = s + (bias_ref[h] if width == 2 else bias_ref[h, tk:, :])
            parts.append(s)
        m_new, alpha, p = _softmax_step(jnp.concatenate(parts, axis=1), m_ref[...])
        vt = jnp.concatenate([svt_ref[j0 + w] for w in range(width)], axis=1)
        acc_ref[...] = alpha * acc_ref[...] + _dot(vt, p)
        m_ref[...] = m_new

    _run_tiles(i, scores, attend)

    acc = acc_ref[...]
    o = acc[:HEAD_DIM, :] / acc[HEAD_DIM:HEAD_DIM + 1, :]
    for pr in range(DSA_HEADS // 2):
        pair = jnp.concatenate([o[:, 2 * pr * tq:(2 * pr + 1) * tq], o[:, (2 * pr + 1) * tq:(2 * pr + 2) * tq]],
                               axis=0)
        o_ref[:, pr * LANES:(pr + 1) * LANES] = pair.T.astype(BF16)


def _attn_b(p, bias, vis, *, batch, nq, tq, tkeys, q_tile_index, n_sel):
    tk = p["svt"].shape[2]
    nkt = tkeys // tk
    assert (tkeys + tk) // COUNT_ROWS <= 256, "bf16 partial counts must stay exact"
    kern = functools.partial(_attn_b_kernel, q_tile_index=q_tile_index, n_sel=n_sel)
    tri = jnp.tril(jnp.ones((tk, tk), jnp.bfloat16))
    qrow = lambda b, i: (b * nq + i, 0)
    krow = lambda b, i: (b, 0)
    return pl.pallas_call(
        kern, grid=(batch, nq),
        in_specs=[pl.BlockSpec((tq, IDX_HEADS * IDX_DIM), qrow),
                  pl.BlockSpec((tkeys, LANES), krow),
                  pl.BlockSpec((8, tq), lambda b, i: (0, b * nq + i)),
                  pl.BlockSpec((tq, DSA_WIDTH), qrow),
                  pl.BlockSpec((tkeys, LANES), krow),
                  pl.BlockSpec((nkt, VB_ROWS, tk), lambda b, i: (b, 0, 0)),
                  _const_spec(bias.shape), _const_spec(vis.shape), _const_spec(tri.shape)],
        out_specs=pl.BlockSpec((tq, DSA_WIDTH), qrow),
        out_shape=jax.ShapeDtypeStruct((batch * nq * tq, DSA_WIDTH), BF16),
        scratch_shapes=[pltpu.VMEM((tkeys + tk, tq), F32),
                        pltpu.VMEM((tkeys + tk, tq), jnp.bfloat16),
                        pltpu.VMEM((tkeys + tk, tq), jnp.int16),
                        pltpu.VMEM((1, DSA_HEADS * tq), F32),
                        pltpu.VMEM((VB_ROWS, DSA_HEADS * tq), F32),
                        pltpu.VMEM((1, tq), F32)],
        compiler_params=_params(2), name="attn_b")(
            p["iq"], p["ikk"], p["iwt"], p["sq"], p["skk"], p["svt"], bias, vis, tri)


def _t5_bucket(rel):
    half = NUM_BUCKETS // 2
    exact = half // 2
    n = jnp.abs(rel)
    logn = jnp.log(jnp.maximum(n, 1).astype(F32) / exact)
    large = exact + (logn / math.log(MAX_DISTANCE / exact) * (half - exact)).astype(I32)
    large = jnp.minimum(large, half - 1)
    return jnp.where(rel > 0, half, 0) + jnp.where(n < exact, n, large)


def _bias_lookup(rel_bias, bucket):
    out = jnp.zeros((rel_bias.shape[1],) + bucket.shape, F32)
    for b in range(NUM_BUCKETS):
        out = out + jnp.where(bucket[None] == b, rel_bias[b].astype(F32)[:, None, None], 0.0)
    return out


def _bias_tiles(rel_bias, tk, tq, q_valid, k_valid_diag):
    kk = jnp.arange(tk, dtype=I32)[:, None]
    qq = jnp.arange(tq, dtype=I32)[None, :]
    far = _bias_lookup(rel_bias, _t5_bucket(jnp.full((1, 1), -(2 * MAX_DISTANCE), I32)))
    tiles = [(_bias_lookup(rel_bias, _t5_bucket(kk + d - qq)) - far) * LOG2E for d in (-tk, 0)]
    vis = ((kk // CHUNK) <= (qq // CHUNK)) & (kk < k_valid_diag) | (qq >= q_valid)
    tiles[1] = jnp.where(vis[None], tiles[1], M_INIT)
    return jnp.concatenate(tiles, axis=1).astype(BF16), vis.astype(F32)


def _token_mix(p, rel_bias, lamv, gsub, *, batch, nq, tq, tkeys, q_tile_index, q_valid, k_valid_diag,
               n_sel, lam_init):
    bias, vis = _bias_tiles(rel_bias, KEY_TILE, tq, q_valid, k_valid_diag)
    gsub_b = jnp.broadcast_to(gsub.astype(F32)[:, None], (DIFF_V_DIM, tq))
    o_a = _attn_a(lamv, p["dq"], p["dk"], p["dvt"], bias[:DIFF_HEADS], gsub_b, batch=batch, nq=nq, tq=tq,
                  tkeys=tkeys, q_tile_index=q_tile_index, lam_init=lam_init)
    o_b = _attn_b(p, bias[DIFF_HEADS:], vis, batch=batch, nq=nq, tq=tq, tkeys=tkeys,
                  q_tile_index=q_tile_index, n_sel=n_sel)
    return o_a, o_b


def _pad_rows(a, rows):
    return jnp.pad(a, ((0, 0), (0, rows - a.shape[1])) + ((0, 0),) * (a.ndim - 2))


def kernel(x_prompt, x_sample, cache_diff_k, cache_diff_v, cache_dsa_k, cache_dsa_v, cache_idx_k, g_ffn1, w1_gate, w1_up, w1_down, g_mix, w_in, g_dq, g_dk, g_sq, g_sk, lam_q1, lam_k1, lam_q2, lam_k2, g_subln, w_out, g_ffn2, w2_gate, w2_up, w2_down, rel_bias):
    depth = g_ffn1.shape[0]
    assert depth == 1, "one layer"
    l = 0
    batch, seq, d_model = x_prompt.shape
    sbatch, sseq, _ = x_sample.shape
    past = cache_diff_k.shape[2]
    tk = KEY_TILE
    assert seq % Q_TILE == 0 and Q_TILE % CHUNK == 0 and Q_TILE == tk
    assert past % tk == 0 and past % CHUNK == 0 and sseq <= CHUNK and sseq % 8 == 0
    lam_init = 0.8 - 0.6 * math.exp(-0.3 * l)

    bf = lambda w: w.astype(BF16)
    row = lambda g: g.astype(F32)[None, :]
    w1 = (row(g_ffn1[l]), bf(w1_gate[l]), bf(w1_up[l]), bf(w1_down[l]))
    w2 = (row(g_ffn2[l]), bf(w2_gate[l]), bf(w2_up[l]), bf(w2_down[l]))
    w_in_p = jnp.pad(bf(w_in[l]), ((0, 0), (0, IN_COLS_PAD - IN_COLS)))
    wo_a, wo_b = bf(w_out[l][:DIFF_WIDTH]), bf(w_out[l][DIFF_WIDTH:])
    lamv = jnp.pad(jnp.stack([lam_q1[l], lam_k1[l], lam_q2[l], lam_k2[l]]).astype(F32), ((0, 4), (0, LANES - HEAD_DIM)))
    proj = functools.partial(_proj, g_mix=row(g_mix[l]), w_in=w_in_p, g_dq=g_dq[l], g_dk=g_dk[l], g_sq=g_sq[l],
                             g_sk=g_sk[l])

    xp1 = _ffn(x_prompt.reshape(batch * seq, d_model), *w1)
    pp = proj(xp1, batch=batch)
    oa_p, ob_p = _token_mix(pp, rel_bias, lamv, g_subln[l], batch=batch, nq=seq // Q_TILE, tq=Q_TILE, tkeys=seq,
                            q_tile_index=None, q_valid=Q_TILE, k_valid_diag=tk,
                            n_sel=min(TOPK_MAX, seq // 4), lam_init=lam_init)
    yp = _out_ffn(xp1, oa_p, ob_p, wo_a, wo_b, *w2).reshape(batch, seq, d_model)

    ns = sbatch * sseq
    xs1 = _ffn(x_sample.reshape(ns, d_model), *w1)
    ps = proj(xs1, batch=1)
    tq_s = LANES
    keys_s = past + tk
    per_b = lambda a: a.reshape(sbatch, sseq, a.shape[-1])

    def q_rows(a):
        return _pad_rows(per_b(a), tq_s).reshape(sbatch * tq_s, a.shape[-1])

    def k_rows(cache, new):
        both = jnp.concatenate([cache.astype(BF16), per_b(new)], axis=1)
        return _pad_rows(both, keys_s).reshape(sbatch * keys_s, new.shape[-1])

    def v_tiles(cache, new_t, rows, heads):
        rv = rows - ONES_ROWS
        c = cache.astype(BF16).reshape(sbatch, past // tk, tk, heads, rv)
        c = jnp.transpose(c, (0, 1, 3, 4, 2))
        c = jnp.concatenate([c, jnp.ones((sbatch, past // tk, heads, ONES_ROWS, tk), BF16)], axis=3)
        c = c.reshape(sbatch, past // tk, heads * rows, tk)
        n = new_t.reshape(heads * rows, sbatch, sseq)
        n = jnp.pad(jnp.transpose(n, (1, 0, 2)), ((0, 0), (0, 0), (0, tk - sseq)))[:, None]
        return jnp.concatenate([c, n], axis=1).reshape(sbatch * (past // tk + 1), heads * rows, tk)

    dup = lambda c: jnp.concatenate([c, c], axis=-1)
    iwt_s = jnp.pad(ps["iwt"].reshape(8, sbatch, sseq), ((0, 0), (0, 0), (0, tq_s - sseq))).reshape(8, sbatch * tq_s)
    psa = {
        "dq": q_rows(ps["dq"]), "sq": q_rows(ps["sq"]), "iq": q_rows(ps["iq"]), "iwt": iwt_s,
        "dk": k_rows(cache_diff_k[l].reshape(sbatch, past, DIFF_QK), ps["dk"]),
        "skk": k_rows(dup(cache_dsa_k[l]), ps["skk"]),
        "ikk": k_rows(dup(cache_idx_k[l]), ps["ikk"]),
        "dvt": v_tiles(cache_diff_v[l].reshape(sbatch, past, DIFF_WIDTH), ps["dvt"], VA_ROWS, DIFF_HEADS),
        "svt": v_tiles(cache_dsa_v[l], ps["svt"], VB_ROWS, 1),
    }
    oa_s, ob_s = _token_mix(psa, rel_bias, lamv, g_subln[l], batch=sbatch, nq=1, tq=tq_s, tkeys=keys_s,
                            q_tile_index=past // tk, q_valid=sseq, k_valid_diag=sseq,
                            n_sel=min(TOPK_MAX, (past + sseq) // 4), lam_init=lam_init)
    take = lambda o: o.reshape(sbatch, tq_s, o.shape[-1])[:, :sseq].reshape(ns, o.shape[-1])
    ys = _out_ffn(xs1, take(oa_s), take(ob_s), wo_a, wo_b, *w2).reshape(sbatch, sseq, d_model)

    def caches(p, b, t):
        def token_major(a):
            feats = a.shape[1]
            a = jnp.moveaxis(a.reshape(a.shape[0], feats, -1, t), 1, -1)
            return a.reshape(depth, b, t, feats)
        return (token_major(p["dk_c"]).reshape(depth, b, t, DIFF_HEADS, 2, HEAD_DIM),
                p["dv_c"].reshape(depth, b, t, DIFF_HEADS, DIFF_V_DIM),
                token_major(p["sk_c"]), token_major(p["sv_c"]), token_major(p["ik_c"]))

    return (yp, ys) + caches(pp, batch, seq) + caches(ps, sbatch, sseq)
```
